```python
import jax, jax.numpy as jnp
from jax import lax
import numpy as np

D_MODEL = 1024
BATCH = 16
SEQ = 2048
DEPTH = 2
DEC_BATCH = 128
DEC_SEQ = 1
PAST_LEN = 16384
PAGE_SIZE = 128

LRU_WIDTH = D_MODEL
LRU_BLOCKS = 16
LRU_BLOCK = LRU_WIDTH // LRU_BLOCKS
CONV_W = 4
LRU_C = 8.0
N_HEADS = 16
N_KV = 4
HEAD_DIM = D_MODEL // N_HEADS
GROUP = N_HEADS // N_KV
WINDOW = 128
Q_W = N_HEADS * HEAD_DIM
KV_W = N_KV * HEAD_DIM
N_MEM = 256
N_XHEADS = 4
XHEAD_DIM = D_MODEL // N_XHEADS
XW = N_XHEADS * XHEAD_DIM
N_BRANCH = 3
D_FF = -(-8 * D_MODEL // (3 * 256)) * 256
LN_EPS = 1e-5
ALPHA = (2 * DEPTH) ** 0.25
BETA = (8 * DEPTH) ** -0.25
COL_WIDTHS = (LRU_WIDTH, LRU_WIDTH, Q_W, KV_W, KV_W, XW, N_BRANCH * D_MODEL)
SPLIT_IDX = [int(c) for c in np.cumsum(COL_WIDTHS)[:-1]]
IN_COLS = int(sum(COL_WIDTHS))

kernel_name = "hybrid_rglru_swa_memxattn_decode_step"


def layer_norm(x, g, b):
    xf = x.astype(jnp.float32)
    mu = xf.mean(-1, keepdims=True)
    var = jnp.square(xf - mu).mean(-1, keepdims=True)
    y = (xf - mu) * lax.rsqrt(var + LN_EPS) * g.astype(jnp.float32) + b.astype(jnp.float32)
    return y.astype(x.dtype)


def causal_conv(xb, buf, w, b):
    T = xb.shape[1]
    xc = jnp.concatenate([buf, xb], axis=1)
    y = b + sum(xc[:, k:k + T] * w[k] for k in range(CONV_W))
    return y, xc[:, -(CONV_W - 1):]


def rg_lru(xc, h0, wa, ba, wx, bx, lam):
    B, T, C = xc.shape
    f32 = jnp.float32
    xf = xc.astype(f32)
    xblk = xf.reshape(B, T, LRU_BLOCKS, LRU_BLOCK)
    r = jax.nn.sigmoid(jnp.einsum('btni,nij->btnj', xblk, wa.astype(f32)).reshape(B, T, C) + ba.astype(f32))
    i = jax.nn.sigmoid(jnp.einsum('btni,nij->btnj', xblk, wx.astype(f32)).reshape(B, T, C) + bx.astype(f32))
    log_a = -LRU_C * r * jax.nn.softplus(-lam.astype(f32))
    a = jnp.exp(log_a)
    u = jnp.sqrt(-jnp.expm1(2.0 * log_a)) * (i * xf)

    def step(h, au):
        a_t, u_t = au
        h = a_t * h + u_t
        return h, h

    hT, hs = lax.scan(step, h0.astype(f32), (jnp.swapaxes(a, 0, 1), jnp.swapaxes(u, 0, 1)))
    return jnp.swapaxes(hs, 0, 1).astype(xc.dtype), hT.astype(h0.dtype)


def sink_softmax(s, mask, sinks):
    sk = sinks.astype(jnp.float32).reshape(N_KV, GROUP)[..., None, None]
    s = jnp.where(mask, s, -1e30)
    mx = jnp.maximum(s.max(-1, keepdims=True), sk)
    e = jnp.exp(s - mx)
    return e / (e.sum(-1, keepdims=True) + jnp.exp(sk - mx))


def band_mask(nq, nk_prefix):
    qi = jnp.arange(nq)[:, None]
    m = jnp.arange(nk_prefix + nq)[None, :]
    return (m <= qi + nk_prefix) & (m > qi + nk_prefix - WINDOW)


def swa_prompt(q, k, v, sinks):
    B, T = q.shape[:2]
    nb = T // WINDOW
    scale = HEAD_DIM ** -0.5
    qb = q.reshape(B, nb, WINDOW, N_KV, GROUP, HEAD_DIM)
    kb = k.reshape(B, nb, WINDOW, N_KV, HEAD_DIM)
    vb = v.reshape(B, nb, WINDOW, N_KV, HEAD_DIM)
    pad = ((0, 0), (1, 0), (0, 0), (0, 0), (0, 0))
    kband = jnp.concatenate([jnp.pad(kb[:, :-1], pad), kb], axis=2)
    vband = jnp.concatenate([jnp.pad(vb[:, :-1], pad), vb], axis=2)
    s = jnp.einsum('bnqkgd,bnmkd->bnkgqm', qb, kband).astype(jnp.float32) * scale
    band = band_mask(WINDOW, WINDOW)
    m = jnp.arange(2 * WINDOW)[None, :]
    real = (jnp.arange(nb)[:, None, None] > 0) | (m >= WINDOW)[None]
    mask = (band[None] & real)[None, :, None, None]
    p = sink_softmax(s, mask, sinks).astype(v.dtype)
    o = jnp.einsum('bnkgqm,bnmkd->bnqkgd', p, vband)
    return o.reshape(B, T, Q_W)


def swa_decode(q, k, v, win_k, win_v, sinks):
    B, T = q.shape[:2]
    scale = HEAD_DIM ** -0.5
    kall = jnp.concatenate([win_k, k], axis=1)
    vall = jnp.concatenate([win_v, v], axis=1)
    qg = q.reshape(B, T, N_KV, GROUP, HEAD_DIM)
    s = jnp.einsum('btkgd,bmkd->bkgtm', qg, kall).astype(jnp.float32) * scale
    p = sink_softmax(s, band_mask(T, WINDOW), sinks).astype(v.dtype)
    o = jnp.einsum('bkgtm,bmkd->btkgd', p, vall)
    return o.reshape(B, T, Q_W), kall[:, -WINDOW:], vall[:, -WINDOW:]


def mem_attn(q, mk, mv):
    s = jnp.einsum('bthd,bmhd->bhtm', q, mk).astype(jnp.float32) * (XHEAD_DIM ** -0.5)
    p = jax.nn.softmax(s, axis=-1).astype(mv.dtype)
    B, T = q.shape[:2]
    return jnp.einsum('bhtm,bmhd->bthd', p, mv).reshape(B, T, XW)


def decoder_layer(x, conv_buf, h0, win_k, win_v, mem_k, mem_v, is_prompt, lw):
    (w_in, b_gates, conv_w, conv_b, lru_wa, lru_ba, lru_wx, lru_bx, lru_lambda, sinks,
     w_branch, w_out, ln1_g, ln1_b, w_gate_up, w_down, ln2_g, ln2_b) = lw
    B, T, _ = x.shape
    z = x @ w_in
    xr, yr, q, k, v, xq, g = jnp.split(z, SPLIT_IDX, axis=-1)
    xc, conv_new = causal_conv(xr, conv_buf, conv_w, conv_b)
    hs, h_new = rg_lru(xc, h0, lru_wa, lru_ba, lru_wx, lru_bx, lru_lambda)
    a_out = hs * jax.nn.gelu(yr, approximate=True)
    q = q.reshape(B, T, N_HEADS, HEAD_DIM)
    k = k.reshape(B, T, N_KV, HEAD_DIM)
    v = v.reshape(B, T, N_KV, HEAD_DIM)
    if is_prompt:
        s_out = swa_prompt(q, k, v, sinks)
        wk_new, wv_new = k[:, -WINDOW:], v[:, -WINDOW:]
    else:
        s_out, wk_new, wv_new = swa_decode(q, k, v, win_k, win_v, sinks)
    c_out = mem_attn(xq.reshape(B, T, N_XHEADS, XHEAD_DIM), mem_k, mem_v)
    gates = jax.nn.sigmoid(g.reshape(B, T, N_BRANCH, D_MODEL) + b_gates)
    merged = sum(gates[:, :, n] * (br @ w_branch[n]) for n, br in enumerate((a_out, s_out, c_out)))
    h = layer_norm(ALPHA * x + merged @ w_out, ln1_g, ln1_b)
    gt, up = jnp.split(h @ w_gate_up, 2, axis=-1)
    h = layer_norm(ALPHA * h + (jax.nn.silu(gt) * up) @ w_down, ln2_g, ln2_b)
    return h, conv_new, h_new, wk_new, wv_new


def setup_inputs(seed: int = 0) -> dict:
    key = jax.random.key(seed)
    ks = iter(jax.random.split(key, 40))
    nrm = lambda shape, s=1.0: jax.random.normal(next(ks), shape, jnp.float32) * s
    L = DEPTH
    a_init = jax.random.uniform(next(ks), (L, LRU_WIDTH), jnp.float32, 0.9, 0.999)
    return {
        "x_prompt": nrm((BATCH, SEQ, D_MODEL)),
        "x_sample": nrm((DEC_BATCH, DEC_SEQ, D_MODEL)),
        "state_conv": nrm((L, DEC_BATCH, CONV_W - 1, LRU_WIDTH)),
        "state_h": nrm((L, DEC_BATCH, LRU_WIDTH)),
        "cache_win_k": nrm((L, DEC_BATCH, WINDOW, N_KV, HEAD_DIM)),
        "cache_win_v": nrm((L, DEC_BATCH, WINDOW, N_KV, HEAD_DIM)),
        "cache_mem_k": nrm((L, DEC_BATCH, N_MEM, N_XHEADS, XHEAD_DIM)),
        "cache_mem_v": nrm((L, DEC_BATCH, N_MEM, N_XHEADS, XHEAD_DIM)),
        "mem_prompt": nrm((BATCH, N_MEM, D_MODEL)),
        "w_mem_kv": nrm((L, D_MODEL, 2 * XW), D_MODEL ** -0.5),
        "w_in": nrm((L, D_MODEL, IN_COLS), D_MODEL ** -0.5),
        "b_gates": nrm((L, N_BRANCH, D_MODEL), 0.02),
        "conv_w": nrm((L, CONV_W, LRU_WIDTH), CONV_W ** -0.5),
        "conv_b": nrm((L, LRU_WIDTH), 0.02),
        "lru_wa": nrm((L, LRU_BLOCKS, LRU_BLOCK, LRU_BLOCK), LRU_BLOCK ** -0.5),
        "lru_ba": nrm((L, LRU_WIDTH), 0.02),
        "lru_wx": nrm((L, LRU_BLOCKS, LRU_BLOCK, LRU_BLOCK), LRU_BLOCK ** -0.5),
        "lru_bx": nrm((L, LRU_WIDTH), 0.02),
        "lru_lambda": jnp.log(a_init) - jnp.log1p(-a_init),
        "sinks": nrm((L, N_HEADS), 0.5),
        "w_branch": nrm((L, N_BRANCH, D_MODEL, D_MODEL), D_MODEL ** -0.5),
        "w_out": nrm((L, D_MODEL, D_MODEL), BETA * D_MODEL ** -0.5),
        "ln1_g": 1.0 + nrm((L, D_MODEL), 0.02),
        "ln1_b": nrm((L, D_MODEL), 0.02),
        "w_gate_up": nrm((L, D_MODEL, 2 * D_FF), D_MODEL ** -0.5),
        "w_down": nrm((L, D_FF, D_MODEL), BETA * D_FF ** -0.5),
        "ln2_g": 1.0 + nrm((L, D_MODEL), 0.02),
        "ln2_b": nrm((L, D_MODEL), 0.02),
    }


def reference(x_prompt, x_sample, state_conv, state_h, cache_win_k, cache_win_v, cache_mem_k, cache_mem_v,
              mem_prompt, w_mem_kv, w_in, b_gates, conv_w, conv_b, lru_wa, lru_ba, lru_wx, lru_bx, lru_lambda,
              sinks, w_branch, w_out, ln1_g, ln1_b, w_gate_up, w_down, ln2_g, ln2_b):
    xp, xs = x_prompt, x_sample
    Bp = xp.shape[0]
    p_conv, p_h, p_wk, p_wv, p_mk, p_mv = [], [], [], [], [], []
    s_conv, s_h, s_wk, s_wv = [], [], [], []
    zero_conv = jnp.zeros((Bp, CONV_W - 1, LRU_WIDTH), xp.dtype)
    zero_h = jnp.zeros((Bp, LRU_WIDTH), xp.dtype)
    for l in range(DEPTH):
        lw = (w_in[l], b_gates[l], conv_w[l], conv_b[l], lru_wa[l], lru_ba[l], lru_wx[l], lru_bx[l],
              lru_lambda[l], sinks[l], w_branch[l], w_out[l], ln1_g[l], ln1_b[l], w_gate_up[l], w_down[l],
              ln2_g[l], ln2_b[l])
        mkv = (mem_prompt @ w_mem_kv[l]).reshape(Bp, N_MEM, 2, N_XHEADS, XHEAD_DIM)
        mk, mv = mkv[:, :, 0], mkv[:, :, 1]
        xp, c_new, h_new, wk, wv = decoder_layer(xp, zero_conv, zero_h, None, None, mk, mv, True, lw)
        p_conv.append(c_new); p_h.append(h_new); p_wk.append(wk); p_wv.append(wv)
        p_mk.append(mk); p_mv.append(mv)
        xs, c_new, h_new, wk, wv = decoder_layer(xs, state_conv[l], state_h[l], cache_win_k[l], cache_win_v[l],
                                                 cache_mem_k[l], cache_mem_v[l], False, lw)
        s_conv.append(c_new); s_h.append(h_new); s_wk.append(wk); s_wv.append(wv)
    return (xp, xs,
            jnp.stack(p_conv), jnp.stack(p_h), jnp.stack(p_wk), jnp.stack(p_wv), jnp.stack(p_mk), jnp.stack(p_mv),
            jnp.stack(s_conv), jnp.stack(s_h), jnp.stack(s_wk), jnp.stack(s_wv))
```

```python
import functools

import jax
import jax.numpy as jnp
from jax import lax
from jax.experimental import pallas as pl
from jax.experimental.pallas import tpu as pltpu

F32 = jnp.float32
BF16 = jnp.bfloat16

D_MODEL = 1024
DEPTH = 2
LRU_BLOCKS = 16
LRU_BLOCK = D_MODEL // LRU_BLOCKS
CONV_W = 4
LRU_C = 8.0
N_HEADS = 16
N_KV = 4
HEAD_DIM = D_MODEL // N_HEADS
GROUP = N_HEADS // N_KV
WINDOW = 128
KV_W = N_KV * HEAD_DIM
N_MEM = 256
N_XHEADS = 4
XHEAD_DIM = D_MODEL // N_XHEADS
N_BRANCH = 3
D_FF = -(-8 * D_MODEL // (3 * 256)) * 256
LN_EPS = 1e-5
ALPHA = (2 * DEPTH) ** 0.25
IN_COLS = 3 * D_MODEL + 2 * KV_W + D_MODEL + N_BRANCH * D_MODEL
C_XR, C_YR, C_Q, C_K, C_V, C_XQ, C_G = 0, 1024, 2048, 3072, 3328, 3584, 4608

V7X_LANES = 128
V7X_SUBLANES = 8
V7X_MXU_DIM = 256
V7X_VMEM_BYTES = 64 * 1024 * 1024
V7X_VMEM_REQUEST_CAP = 60000 * 1024

LRU_GROUP = V7X_MXU_DIM // LRU_BLOCK
N_LRU_GROUPS = D_MODEL // V7X_MXU_DIM
HALF = V7X_LANES // 2

V_CONV_W, V_CONV_B, V_BA, V_BX, V_LAM, V_BG, V_LN1G, V_LN1B, V_LN2G, V_LN2B = 0, 4, 5, 6, 7, 8, 11, 12, 13, 14
N_VEC = 16

NEG_INF = -1e30


def _dot(a, b):
    return jnp.dot(a, b, preferred_element_type=F32)


def _dot_nt(a, b):
    return lax.dot_general(a, b, (((1,), (1,)), ((), ())), preferred_element_type=F32)


def _layer_norm(y, g, b):
    mu = jnp.mean(y, axis=-1, keepdims=True)
    yc = y - mu
    var = jnp.mean(yc * yc, axis=-1, keepdims=True)
    return yc * lax.rsqrt(var + LN_EPS) * g + b


def _gelu_tanh(x):
    return 0.5 * x * (1.0 + jnp.tanh(0.7978845608028654 * (x + 0.044715 * (x * x * x))))


def _softplus(x):
    return jnp.maximum(x, 0.0) + jnp.log1p(jnp.exp(-jnp.abs(x)))


def _lru_gates(xc, wri_ref, vec_ref, j):
    sl = slice(j * V7X_MXU_DIM, (j + 1) * V7X_MXU_DIM)
    xcj = xc[:, sl]
    ri = _dot(xcj.astype(BF16), wri_ref[j])
    r = jax.nn.sigmoid(ri[:, :V7X_MXU_DIM] + vec_ref[V_BA:V_BA + 1, sl])
    i = jax.nn.sigmoid(ri[:, V7X_MXU_DIM:] + vec_ref[V_BX:V_BX + 1, sl])
    log_a = (-LRU_C) * r * _softplus(-vec_ref[V_LAM:V_LAM + 1, sl])
    a = jnp.exp(log_a)
    u = jnp.sqrt(1.0 - a * a) * (i * xcj)
    return a, u


def _merge_tail(x, merged, wout_ref, vec_ref):
    y = _dot(merged.astype(BF16), wout_ref[...]) + ALPHA * x
    return _layer_norm(y, vec_ref[V_LN1G:V_LN1G + 1, :], vec_ref[V_LN1B:V_LN1B + 1, :])


def _ffn(h, wgu_ref, wd_ref, vec_ref):
    gu = _dot(h.astype(BF16), wgu_ref[...])
    gt, up = gu[:, :D_FF], gu[:, D_FF:]
    act = (gt * jax.nn.sigmoid(gt)) * up
    y = _dot(act.astype(BF16), wd_ref[...]) + ALPHA * h
    return _layer_norm(y, vec_ref[V_LN2G:V_LN2G + 1, :], vec_ref[V_LN2B:V_LN2B + 1, :])


def _mem_kv_kernel(m_ref, w_ref, k_ref, v_ref, kb_ref, vb_ref):
    kv = _dot(m_ref[...].astype(BF16), w_ref[...])
    k, v = kv[:, :D_MODEL], kv[:, D_MODEL:]
    k_ref[...] = k
    v_ref[...] = v
    kb_ref[...] = k.astype(BF16)
    vb_ref[...] = v.astype(BF16)


def _mem_kv(mem2d, w_mem_b, tm):
    rows = mem2d.shape[0]
    nt = rows // tm
    out_sds = jax.ShapeDtypeStruct((DEPTH, rows, D_MODEL), F32)
    out_b = jax.ShapeDtypeStruct((DEPTH, rows, D_MODEL), BF16)
    o_spec = pl.BlockSpec((None, tm, D_MODEL), lambda l, i: (l, i, 0))
    vmem = 2 * (tm * D_MODEL * 4 + D_MODEL * 2 * D_MODEL * 2 + 2 * tm * D_MODEL * 6) + 4 * tm * D_MODEL * 4
    return pl.pallas_call(
        _mem_kv_kernel,
        grid=(DEPTH, nt),
        in_specs=[pl.BlockSpec((tm, D_MODEL), lambda l, i: (i, 0)),
                  pl.BlockSpec((None, D_MODEL, 2 * D_MODEL), lambda l, i: (l, 0, 0))],
        out_specs=[o_spec, o_spec, o_spec, o_spec],
        out_shape=[out_sds, out_sds, out_b, out_b],
        compiler_params=pltpu.CompilerParams(dimension_semantics=("arbitrary", "arbitrary"),
                                             vmem_limit_bytes=min(vmem, V7X_VMEM_REQUEST_CAP)),
        name="mem_kv_proj",
    )(mem2d, w_mem_b)


def _prompt_layer_kernel(x_ref, mk_ref, mv_ref, w_in_ref, wri_ref, wbr_ref, wout_ref, vec_ref, sinks_ref,
                         h1_ref, pconv_ref, ph_ref, pwk_ref, pwv_ref,
                         xr_buf, a_s, u_s, h_s, h_carry, kb, vb, so_s, *, tm):
    t = pl.program_id(1)
    nt = pl.num_programs(1)

    @pl.when(t == 0)
    def _():
        xr_buf[0:V7X_SUBLANES, :] = jnp.zeros((V7X_SUBLANES, D_MODEL), F32)
        h_carry[...] = jnp.zeros((1, D_MODEL), F32)
        kb[0:WINDOW, :] = jnp.zeros((WINDOW, KV_W), BF16)
        vb[0:WINDOW, :] = jnp.zeros((WINDOW, KV_W), BF16)

    x = x_ref[...]
    xb = x.astype(BF16)

    def proj(c0, n):
        return _dot(xb, w_in_ref[:, c0:c0 + n])

    xr = proj(C_XR, D_MODEL)
    xr_buf[V7X_SUBLANES:V7X_SUBLANES + tm, :] = xr
    xc = vec_ref[V_CONV_B:V_CONV_B + 1, :] + vec_ref[V_CONV_W + 3:V_CONV_W + 4, :] * xr
    for k in range(CONV_W - 1):
        off = V7X_SUBLANES - (CONV_W - 1) + k
        xc = xc + vec_ref[V_CONV_W + k:V_CONV_W + k + 1, :] * xr_buf[off:off + tm, :]
    xr_buf[0:V7X_SUBLANES, :] = xr_buf[tm:tm + V7X_SUBLANES, :]
    pconv_ref[...] = xr_buf[0:V7X_SUBLANES, :]

    for j in range(N_LRU_GROUPS):
        sl = slice(j * V7X_MXU_DIM, (j + 1) * V7X_MXU_DIM)
        a, u = _lru_gates(xc, wri_ref, vec_ref, j)
        a_s[:, sl] = a
        u_s[:, sl] = u

    def scan_step(r, h):
        h = a_s[pl.ds(r, 1), :] * h + u_s[pl.ds(r, 1), :]
        h_s[pl.ds(r, 1), :] = h
        return h

    h_last = lax.fori_loop(0, tm, scan_step, h_carry[...], unroll=8)
    h_carry[...] = h_last
    ph_ref[...] = h_last

    a_out = h_s[...] * _gelu_tanh(proj(C_YR, D_MODEL))
    gate = jax.nn.sigmoid(proj(C_G, D_MODEL) + vec_ref[V_BG:V_BG + 1, :])
    merged = gate * _dot(a_out.astype(BF16), wbr_ref[0])

    q = (proj(C_Q, D_MODEL) * (HEAD_DIM ** -0.5)).astype(BF16)
    k = proj(C_K, KV_W)
    v = proj(C_V, KV_W)
    kb[WINDOW:WINDOW + tm, :] = k.astype(BF16)
    vb[WINDOW:WINDOW + tm, :] = v.astype(BF16)

    @pl.when(t == nt - 1)
    def _():
        pwk_ref[...] = k[tm - WINDOW:, :]
        pwv_ref[...] = v[tm - WINDOW:, :]

    rows = GROUP * WINDOW
    qi = lax.broadcasted_iota(jnp.int32, (rows, 2 * WINDOW), 0) & (WINDOW - 1)
    kj = lax.broadcasted_iota(jnp.int32, (rows, 2 * WINDOW), 1)
    band = (kj > qi) & (kj <= qi + WINDOW)
    first_band = band & ((kj >= WINDOW) | (t > 0))
    lo_lanes = lax.broadcasted_iota(jnp.int32, (2 * WINDOW, V7X_LANES), 1) < HALF
    g_of_row = lax.broadcasted_iota(jnp.int32, (rows, 1), 0) >> (WINDOW.bit_length() - 1)
    zero_b = jnp.zeros((2 * WINDOW, V7X_LANES), BF16)

    for n in range(tm // WINDOW):
        mask = first_band if n == 0 else band
        r0 = n * WINDOW
        for p in range(N_KV // 2):
            c0 = p * V7X_LANES
            kp = kb[r0:r0 + 2 * WINDOW, c0:c0 + V7X_LANES]
            vp = vb[r0:r0 + 2 * WINDOW, c0:c0 + V7X_LANES]
            qst = jnp.concatenate(
                [q[r0:r0 + WINDOW, g * V7X_MXU_DIM + c0:g * V7X_MXU_DIM + c0 + V7X_LANES] for g in range(GROUP)],
                axis=0)
            o_pair = None
            for half in range(2):
                keep = lo_lanes if half == 0 else jnp.logical_not(lo_lanes)
                kx = jnp.where(keep, kp, zero_b)
                vx = jnp.where(keep, vp, zero_b)
                head0 = (2 * p + half) * GROUP
                sink = jnp.zeros((rows, 1), F32)
                for g in range(GROUP):
                    sink = jnp.where(g_of_row == g, sinks_ref[head0 + g], sink)
                s = jnp.where(mask, _dot_nt(qst, kx), NEG_INF)
                m = jnp.maximum(jnp.max(s, axis=-1, keepdims=True), sink)
                e = jnp.exp(s - m)
                denom = jnp.sum(e, axis=-1, keepdims=True) + jnp.exp(sink - m)
                o = _dot(e.astype(BF16), vx) / denom
                o_pair = o if o_pair is None else o_pair + o
            for g in range(GROUP):
                so_s[r0:r0 + WINDOW, g * V7X_MXU_DIM + c0:g * V7X_MXU_DIM + c0 + V7X_LANES] = (
                    o_pair[g * WINDOW:(g + 1) * WINDOW, :].astype(BF16))

    kb[0:WINDOW, :] = kb[tm:tm + WINDOW, :]
    vb[0:WINDOW, :] = vb[tm:tm + WINDOW, :]

    gate = jax.nn.sigmoid(proj(C_G + D_MODEL, D_MODEL) + vec_ref[V_BG + 1:V_BG + 2, :])
    merged = merged + gate * _dot(so_s[...], wbr_ref[1])

    xq = (proj(C_XQ, D_MODEL) * (XHEAD_DIM ** -0.5)).astype(BF16)
    br = None
    for h in range(N_XHEADS):
        sl = slice(h * XHEAD_DIM, (h + 1) * XHEAD_DIM)
        s = _dot_nt(xq[:, sl], mk_ref[:, sl])
        m = jnp.max(s, axis=-1, keepdims=True)
        e = jnp.exp(s - m)
        o = _dot(e.astype(BF16), mv_ref[:, sl]) / jnp.sum(e, axis=-1, keepdims=True)
        part = _dot(o.astype(BF16), wbr_ref[2, sl, :])
        br = part if br is None else br + part
    gate = jax.nn.sigmoid(proj(C_G + 2 * D_MODEL, D_MODEL) + vec_ref[V_BG + 2:V_BG + 3, :])
    merged = merged + gate * br

    h1_ref[...] = _merge_tail(x, merged, wout_ref, vec_ref)


def _prompt_layer(xp, mk_b, mv_b, w_in_b, wri_b, wbr_b, wout_b, vecs, sinks, layer, tm):
    batch, seq, _ = xp.shape
    nt = seq // tm
    const = dict(pipeline_mode=pl.Buffered(1))
    in_specs = [
        pl.BlockSpec((None, tm, D_MODEL), lambda b, t: (b, t, 0)),
        pl.BlockSpec((None, None, N_MEM, D_MODEL), lambda b, t: (layer, b, 0, 0)),
        pl.BlockSpec((None, None, N_MEM, D_MODEL), lambda b, t: (layer, b, 0, 0)),
        pl.BlockSpec((None, D_MODEL, IN_COLS), lambda b, t: (layer, 0, 0), **const),
        pl.BlockSpec((None, N_LRU_GROUPS, V7X_MXU_DIM, 2 * V7X_MXU_DIM), lambda b, t: (layer, 0, 0, 0), **const),
        pl.BlockSpec((None, N_BRANCH, D_MODEL, D_MODEL), lambda b, t: (layer, 0, 0, 0), **const),
        pl.BlockSpec((None, D_MODEL, D_MODEL), lambda b, t: (layer, 0, 0), **const),
        pl.BlockSpec((None, N_VEC, D_MODEL), lambda b, t: (layer, 0, 0), **const),
        pl.BlockSpec(memory_space=pltpu.SMEM),
    ]
    out_specs = [
        pl.BlockSpec((None, tm, D_MODEL), lambda b, t: (b, t, 0)),
        pl.BlockSpec((None, V7X_SUBLANES, D_MODEL), lambda b, t: (b, 0, 0)),
        pl.BlockSpec((None, 1, D_MODEL), lambda b, t: (b, 0, 0)),
        pl.BlockSpec((None, WINDOW, KV_W), lambda b, t: (b, 0, 0)),
        pl.BlockSpec((None, WINDOW, KV_W), lambda b, t: (b, 0, 0)),
    ]
    out_shape = [
        jax.ShapeDtypeStruct((batch, seq, D_MODEL), F32),
        jax.ShapeDtypeStruct((batch, V7X_SUBLANES, D_MODEL), F32),
        jax.ShapeDtypeStruct((batch, 1, D_MODEL), F32),
        jax.ShapeDtypeStruct((batch, WINDOW, KV_W), F32),
        jax.ShapeDtypeStruct((batch, WINDOW, KV_W), F32),
    ]
    scratch = [
        pltpu.VMEM((tm + V7X_SUBLANES, D_MODEL), F32),
        pltpu.VMEM((tm, D_MODEL), F32),
        pltpu.VMEM((tm, D_MODEL), F32),
        pltpu.VMEM((tm, D_MODEL), F32),
        pltpu.VMEM((1, D_MODEL), F32),
        pltpu.VMEM((tm + WINDOW, KV_W), BF16),
        pltpu.VMEM((tm + WINDOW, KV_W), BF16),
        pltpu.VMEM((tm, D_MODEL), BF16),
    ]
    weights = (D_MODEL * IN_COLS + N_LRU_GROUPS * V7X_MXU_DIM * 2 * V7X_MXU_DIM
               + (N_BRANCH + 1) * D_MODEL * D_MODEL) * 2
    tile = tm * D_MODEL * 4
    vmem = weights + 4 * N_MEM * D_MODEL * 2 + 4 * tile + 5 * tile + 16 * tile
    return pl.pallas_call(
        functools.partial(_prompt_layer_kernel, tm=tm),
        grid=(batch, nt),
        in_specs=in_specs,
        out_specs=out_specs,
        out_shape=out_shape,
        scratch_shapes=scratch,
        compiler_params=pltpu.CompilerParams(dimension_semantics=("arbitrary", "arbitrary"),
                                             vmem_limit_bytes=min(vmem, V7X_VMEM_REQUEST_CAP)),
        name=f"prompt_layer_{layer}",
    )(xp, mk_b, mv_b, w_in_b, wri_b, wbr_b, wout_b, vecs, sinks)


def _ffn_kernel(h_ref, wgu_ref, wd_ref, vec_ref, o_ref):
    o_ref[...] = _ffn(h_ref[...], wgu_ref, wd_ref, vec_ref)


def _ffn_call(h2d, wgu_b, wd_b, vecs, layer, tm):
    rows = h2d.shape[0]
    const = dict(pipeline_mode=pl.Buffered(1))
    weights = 3 * D_MODEL * D_FF * 2
    tile = tm * D_MODEL * 4
    vmem = weights + 4 * tile + 3 * tm * 2 * D_FF * 4 + 4 * tile
    return pl.pallas_call(
        _ffn_kernel,
        grid=(rows // tm,),
        in_specs=[pl.BlockSpec((tm, D_MODEL), lambda i: (i, 0)),
                  pl.BlockSpec((None, D_MODEL, 2 * D_FF), lambda i: (layer, 0, 0), **const),
                  pl.BlockSpec((None, D_FF, D_MODEL), lambda i: (layer, 0, 0), **const),
                  pl.BlockSpec((None, N_VEC, D_MODEL), lambda i: (layer, 0, 0), **const)],
        out_specs=pl.BlockSpec((tm, D_MODEL), lambda i: (i, 0)),
        out_shape=jax.ShapeDtypeStruct((rows, D_MODEL), F32),
        compiler_params=pltpu.CompilerParams(dimension_semantics=("arbitrary",),
                                             vmem_limit_bytes=min(vmem, V7X_VMEM_REQUEST_CAP)),
        name=f"prompt_ffn_{layer}",
    )(h2d, wgu_b, wd_b, vecs)


def _decode_proj_kernel(x_ref, sc_ref, h0_ref, w_in_ref, wri_ref, vec_ref,
                        aout_ref, q_ref, k_ref, v_ref, xq_ref, gate_ref, sconv_ref, sh_ref):
    x = x_ref[...]
    xb = x.astype(BF16)

    def proj(c0, n):
        return _dot(xb, w_in_ref[:, c0:c0 + n])

    xr = proj(C_XR, D_MODEL)
    xc = vec_ref[V_CONV_B:V_CONV_B + 1, :] + vec_ref[V_CONV_W + 3:V_CONV_W + 4, :] * xr
    for k in range(CONV_W - 1):
        xc = xc + vec_ref[V_CONV_W + k:V_CONV_W + k + 1, :] * sc_ref[:, k * D_MODEL:(k + 1) * D_MODEL]
    sconv_ref[:, 0:(CONV_W - 2) * D_MODEL] = sc_ref[:, D_MODEL:(CONV_W - 1) * D_MODEL]
    sconv_ref[:, (CONV_W - 2) * D_MODEL:] = xr

    gelu_y = _gelu_tanh(proj(C_YR, D_MODEL))
    for j in range(N_LRU_GROUPS):
        sl = slice(j * V7X_MXU_DIM, (j + 1) * V7X_MXU_DIM)
        a, u = _lru_gates(xc, wri_ref, vec_ref, j)
        h = a * h0_ref[:, sl] + u
        sh_ref[:, sl] = h
        aout_ref[:, sl] = h * gelu_y[:, sl]

    q_ref[...] = proj(C_Q, D_MODEL)
    k_ref[...] = proj(C_K, KV_W)
    v_ref[...] = proj(C_V, KV_W)
    xq_ref[...] = proj(C_XQ, D_MODEL)
    for n in range(N_BRANCH):
        sl = slice(n * D_MODEL, (n + 1) * D_MODEL)
        gate_ref[:, sl] = jax.nn.sigmoid(proj(C_G + n * D_MODEL, D_MODEL) + vec_ref[V_BG + n:V_BG + n + 1, :])


def _decode_proj(xs, sc, h0, w_in_b, wri_b, vecs, layer):
    b = xs.shape[0]
    sds = lambda n: jax.ShapeDtypeStruct((b, n), F32)
    full = lambda n: pl.BlockSpec((b, n), lambda i: (0, 0))
    weights = (D_MODEL * IN_COLS + N_LRU_GROUPS * V7X_MXU_DIM * 2 * V7X_MXU_DIM) * 2
    vmem = weights + 2 * 16 * b * D_MODEL * 4 + 8 * b * D_MODEL * 4
    return pl.pallas_call(
        _decode_proj_kernel,
        grid=(1,),
        in_specs=[full(D_MODEL), full((CONV_W - 1) * D_MODEL), full(D_MODEL),
                  pl.BlockSpec((None, D_MODEL, IN_COLS), lambda i: (layer, 0, 0)),
                  pl.BlockSpec((None, N_LRU_GROUPS, V7X_MXU_DIM, 2 * V7X_MXU_DIM), lambda i: (layer, 0, 0, 0)),
                  pl.BlockSpec((None, N_VEC, D_MODEL), lambda i: (layer, 0, 0))],
        out_specs=[full(D_MODEL), full(D_MODEL), full(KV_W), full(KV_W), full(D_MODEL),
                   full(N_BRANCH * D_MODEL), full((CONV_W - 1) * D_MODEL), full(D_MODEL)],
        out_shape=[sds(D_MODEL), sds(D_MODEL), sds(KV_W), sds(KV_W), sds(D_MODEL),
                   sds(N_BRANCH * D_MODEL), sds((CONV_W - 1) * D_MODEL), sds(D_MODEL)],
        compiler_params=pltpu.CompilerParams(dimension_semantics=("arbitrary",),
                                             vmem_limit_bytes=min(vmem, V7X_VMEM_REQUEST_CAP)),
        name=f"decode_proj_{layer}",
    )(xs, sc, h0, w_in_b, wri_b, vecs)


def _decode_attn_kernel(q16_ref, knew_ref, vnew_ref, wk_ref, wv_ref, xq_ref, mk_ref, mv_ref, sink_ref,
                        owk_ref, owv_ref, so_ref, co_ref, *, bt):
    row = lax.broadcasted_iota(jnp.int32, (WINDOW, KV_W), 0)
    head_row = lax.broadcasted_iota(jnp.int32, (N_HEADS, KV_W), 0)
    head_lane = lax.broadcasted_iota(jnp.int32, (N_HEADS, KV_W), 1)
    own = (head_lane >> (HEAD_DIM.bit_length() - 1)) == (head_row >> (GROUP.bit_length() - 1))
    sink = sink_ref[...]
    for i in range(bt):
        kwin = jnp.where(row == WINDOW - 1, knew_ref[i], pltpu.roll(wk_ref[i], WINDOW - 1, 0))
        vwin = jnp.where(row == WINDOW - 1, vnew_ref[i], pltpu.roll(wv_ref[i], WINDOW - 1, 0))
        owk_ref[i] = kwin
        owv_ref[i] = vwin
        q = jnp.where(own, q16_ref[i] * (HEAD_DIM ** -0.5), 0.0).astype(BF16)
        s = _dot_nt(q, kwin.astype(BF16))
        m = jnp.maximum(jnp.max(s, axis=-1, keepdims=True), sink)
        e = jnp.exp(s - m)
        denom = jnp.sum(e, axis=-1, keepdims=True) + jnp.exp(sink - m)
        o = jnp.where(own, _dot(e.astype(BF16), vwin.astype(BF16)) / denom, 0.0)
        acc = o[0:GROUP, :]
        for kk in range(1, N_KV):
            acc = acc + o[kk * GROUP:(kk + 1) * GROUP, :]
        so_ref[i] = acc

        prod = mk_ref[i] * (xq_ref[i] * (XHEAD_DIM ** -0.5))
        for h in range(N_XHEADS):
            sl = slice(h * XHEAD_DIM, (h + 1) * XHEAD_DIM)
            sc = jnp.sum(prod[:, sl], axis=-1, keepdims=True)
            mx = jnp.max(sc, axis=0, keepdims=True)
            ex = jnp.exp(sc - mx)
            pr = ex / jnp.sum(ex, axis=0, keepdims=True)
            co_ref[i, :, sl] = jnp.sum(pr * mv_ref[i, :, sl], axis=0, keepdims=True)


def _decode_attn(q16, knew, vnew, win_k, win_v, xq, mem_k, mem_v, sink_col, layer, bt):
    b = q16.shape[0]
    cache = pl.BlockSpec((None, bt, WINDOW, KV_W), lambda i: (layer, i, 0, 0))
    memsp = pl.BlockSpec((None, bt, N_MEM, D_MODEL), lambda i: (layer, i, 0, 0))
    vmem = 2 * bt * (2 * N_MEM * D_MODEL + 4 * WINDOW * KV_W + 8 * D_MODEL) * 4 + 6 * N_MEM * D_MODEL * 4
    return pl.pallas_call(
        functools.partial(_decode_attn_kernel, bt=bt),
        grid=(b // bt,),
        in_specs=[pl.BlockSpec((bt, N_HEADS, KV_W), lambda i: (i, 0, 0)),
                  pl.BlockSpec((bt, 1, KV_W), lambda i: (i, 0, 0)),
                  pl.BlockSpec((bt, 1, KV_W), lambda i: (i, 0, 0)),
                  cache, cache,
                  pl.BlockSpec((bt, 1, D_MODEL), lambda i: (i, 0, 0)),
                  memsp, memsp,
                  pl.BlockSpec((None, N_HEADS, 1), lambda i: (layer, 0, 0))],
        out_specs=[pl.BlockSpec((bt, WINDOW, KV_W), lambda i: (i, 0, 0)),
                   pl.BlockSpec((bt, WINDOW, KV_W), lambda i: (i, 0, 0)),
                   pl.BlockSpec((bt, GROUP, KV_W), lambda i: (i, 0, 0)),
                   pl.BlockSpec((bt, 1, D_MODEL), lambda i: (i, 0, 0))],
        out_shape=[jax.ShapeDtypeStruct((b, WINDOW, KV_W), F32),
                   jax.ShapeDtypeStruct((b, WINDOW, KV_W), F32),
                   jax.ShapeDtypeStruct((b, GROUP, KV_W), F32),
                   jax.ShapeDtypeStruct((b, 1, D_MODEL), F32)],
        compiler_params=pltpu.CompilerParams(dimension_semantics=("arbitrary",),
                                             vmem_limit_bytes=min(vmem, V7X_VMEM_REQUEST_CAP)),
        name=f"decode_attn_{layer}",
    )(q16, knew, vnew, win_k, win_v, xq, mem_k, mem_v, sink_col)


def _decode_tail_kernel(x_ref, aout_ref, so_ref, co_ref, gate_ref, wbr_ref, wout_ref, wgu_ref, wd_ref, vec_ref,
                        o_ref):
    merged = None
    for n, br_ref in enumerate((aout_ref, so_ref, co_ref)):
        part = gate_ref[:, n * D_MODEL:(n + 1) * D_MODEL] * _dot(br_ref[...].astype(BF16), wbr_ref[n])
        merged = part if merged is None else merged + part
    h = _merge_tail(x_ref[...], merged, wout_ref, vec_ref)
    o_ref[...] = _ffn(h, wgu_ref, wd_ref, vec_ref)


def _decode_tail(xs, aout, so, co, gates, wbr_b, wout_b, wgu_b, wd_b, vecs, layer):
    b = xs.shape[0]
    full = lambda n: pl.BlockSpec((b, n), lambda i: (0, 0))
    weights = ((N_BRANCH + 1) * D_MODEL * D_MODEL + 3 * D_MODEL * D_FF) * 2
    vmem = weights + 2 * 8 * b * D_MODEL * 4 + 3 * b * 2 * D_FF * 4 + 8 * b * D_MODEL * 4
    return pl.pallas_call(
        _decode_tail_kernel,
        grid=(1,),
        in_specs=[full(D_MODEL), full(D_MODEL), full(D_MODEL), full(D_MODEL), full(N_BRANCH * D_MODEL),
                  pl.BlockSpec((None, N_BRANCH, D_MODEL, D_MODEL), lambda i: (layer, 0, 0, 0)),
                  pl.BlockSpec((None, D_MODEL, D_MODEL), lambda i: (layer, 0, 0)),
                  pl.BlockSpec((None, D_MODEL, 2 * D_FF), lambda i: (layer, 0, 0)),
                  pl.BlockSpec((None, D_FF, D_MODEL), lambda i: (layer, 0, 0)),
                  pl.BlockSpec((None, N_VEC, D_MODEL), lambda i: (layer, 0, 0))],
        out_specs=full(D_MODEL),
        out_shape=jax.ShapeDtypeStruct((b, D_MODEL), F32),
        compiler_params=pltpu.CompilerParams(dimension_semantics=("arbitrary",),
                                             vmem_limit_bytes=min(vmem, V7X_VMEM_REQUEST_CAP)),
        name=f"decode_tail_{layer}",
    )(xs, aout, so, co, gates, wbr_b, wout_b, wgu_b, wd_b, vecs)


def _head_major_to_member_major(w, axis):
    shape = w.shape
    w = w.reshape(shape[:axis] + (N_KV, GROUP, HEAD_DIM) + shape[axis + 1:])
    w = jnp.swapaxes(w, axis, axis + 1)
    return w.reshape(shape)


def _block_diag_groups(w):
    depth = w.shape[0]
    w = w.reshape(depth, N_LRU_GROUPS, LRU_GROUP, LRU_BLOCK, LRU_BLOCK)
    eye = jnp.eye(LRU_GROUP, dtype=w.dtype)
    bd = w[:, :, :, :, None, :] * eye[None, None, :, None, :, None]
    return bd.reshape(depth, N_LRU_GROUPS, V7X_MXU_DIM, V7X_MXU_DIM)


def kernel(x_prompt, x_sample, state_conv, state_h, cache_win_k, cache_win_v, cache_mem_k, cache_mem_v, mem_prompt, w_mem_kv, w_in, b_gates, conv_w, conv_b, lru_wa, lru_ba, lru_wx, lru_bx, lru_lambda, sinks, w_branch, w_out, ln1_g, ln1_b, w_gate_up, w_down, ln2_g, ln2_b):
    bp, seq, _ = x_prompt.shape
    bs = x_sample.shape[0]

    w_in_b = jnp.concatenate(
        [w_in[..., :C_Q], _head_major_to_member_major(w_in[..., C_Q:C_K], 2), w_in[..., C_K:]], axis=-1).astype(BF16)
    wbr_b = jnp.stack(
        [w_branch[:, 0], _head_major_to_member_major(w_branch[:, 1], 1), w_branch[:, 2]], axis=1).astype(BF16)
    wri_b = jnp.concatenate([_block_diag_groups(lru_wa), _block_diag_groups(lru_wx)], axis=-1).astype(BF16)
    wout_b = w_out.astype(BF16)
    wgu_b = w_gate_up.astype(BF16)
    wd_b = w_down.astype(BF16)
    wmem_b = w_mem_kv.astype(BF16)
    row = lambda p: p[:, None, :]
    vecs = jnp.concatenate(
        [conv_w, row(conv_b), row(lru_ba), row(lru_bx), row(lru_lambda), b_gates, row(ln1_g), row(ln1_b),
         row(ln2_g), row(ln2_b), jnp.zeros((DEPTH, N_VEC - 15, D_MODEL), F32)], axis=1)
    sinks_flat = sinks.reshape(DEPTH * N_HEADS)
    sink_col = sinks[:, :, None]

    mk, mv, mk_b, mv_b = _mem_kv(mem_prompt.reshape(bp * N_MEM, D_MODEL), wmem_b, tm=512)
    mk_b = mk_b.reshape(DEPTH, bp, N_MEM, D_MODEL)
    mv_b = mv_b.reshape(DEPTH, bp, N_MEM, D_MODEL)

    win_k = cache_win_k.reshape(DEPTH, bs, WINDOW, KV_W)
    win_v = cache_win_v.reshape(DEPTH, bs, WINDOW, KV_W)
    mem_k = cache_mem_k.reshape(DEPTH, bs, N_MEM, D_MODEL)
    mem_v = cache_mem_v.reshape(DEPTH, bs, N_MEM, D_MODEL)

    xp = x_prompt
    xs = x_sample.reshape(bs, D_MODEL)
    p_conv, p_h, p_wk, p_wv = [], [], [], []
    s_conv, s_h, s_wk, s_wv = [], [], [], []
    for l in range(DEPTH):
        h1, pc8, ph, pwk, pwv = _prompt_layer(xp, mk_b, mv_b, w_in_b, wri_b, wbr_b, wout_b, vecs,
                                              sinks_flat[l * N_HEADS:(l + 1) * N_HEADS], l, tm=256)
        xp = _ffn_call(h1.reshape(bp * seq, D_MODEL), wgu_b, wd_b, vecs, l, tm=256).reshape(bp, seq, D_MODEL)
        p_conv.append(pc8[:, V7X_SUBLANES - (CONV_W - 1):, :])
        p_h.append(ph.reshape(bp, D_MODEL))
        p_wk.append(pwk.reshape(bp, WINDOW, N_KV, HEAD_DIM))
        p_wv.append(pwv.reshape(bp, WINDOW, N_KV, HEAD_DIM))

        aout, q, knew, vnew, xq, gates, sconv, sh = _decode_proj(
            xs, state_conv[l].reshape(bs, (CONV_W - 1) * D_MODEL), state_h[l], w_in_b, wri_b, vecs, l)
        q16 = jnp.tile(q.reshape(bs, GROUP, KV_W), (1, N_KV, 1))
        owk, owv, so, co = _decode_attn(q16, knew.reshape(bs, 1, KV_W), vnew.reshape(bs, 1, KV_W), win_k, win_v,
                                        xq.reshape(bs, 1, D_MODEL), mem_k, mem_v, sink_col, l, bt=4)
        xs = _decode_tail(xs, aout, so.reshape(bs, D_MODEL), co.reshape(bs, D_MODEL), gates,
                          wbr_b, wout_b, wgu_b, wd_b, vecs, l)
        s_conv.append(sconv.reshape(bs, CONV_W - 1, D_MODEL))
        s_h.append(sh)
        s_wk.append(owk.reshape(bs, WINDOW, N_KV, HEAD_DIM))
        s_wv.append(owv.reshape(bs, WINDOW, N_KV, HEAD_DIM))

    return (xp, xs.reshape(bs, 1, D_MODEL),
            jnp.stack(p_conv), jnp.stack(p_h), jnp.stack(p_wk), jnp.stack(p_wv),
            mk.reshape(DEPTH, bp, N_MEM, N_XHEADS, XHEAD_DIM), mv.reshape(DEPTH, bp, N_MEM, N_XHEADS, XHEAD_DIM),
            jnp.stack(s_conv), jnp.stack(s_h), jnp.stack(s_wk), jnp.stack(s_wv))
```

```python
import functools

import jax
import jax.numpy as jnp
from jax import lax
from jax.experimental import pallas as pl
from jax.experimental.pallas import tpu as pltpu

F32 = jnp.float32
BF16 = jnp.bfloat16

D_MODEL = 1024
DEPTH = 2
LRU_BLOCKS = 16
LRU_BLOCK = D_MODEL // LRU_BLOCKS
CONV_W = 4
LRU_C = 8.0
N_HEADS = 16
N_KV = 4
HEAD_DIM = D_MODEL // N_HEADS
GROUP = N_HEADS // N_KV
WINDOW = 128
KV_W = N_KV * HEAD_DIM
N_MEM = 256
N_XHEADS = 4
XHEAD_DIM = D_MODEL // N_XHEADS
N_BRANCH = 3
D_FF = -(-8 * D_MODEL // (3 * 256)) * 256
LN_EPS = 1e-5
ALPHA = (2 * DEPTH) ** 0.25
IN_COLS = 3 * D_MODEL + 2 * KV_W + D_MODEL + N_BRANCH * D_MODEL
C_XR, C_YR, C_Q, C_K, C_V, C_XQ, C_G = 0, 1024, 2048, 3072, 3328, 3584, 4608

V7X_LANES = 128
V7X_SUBLANES = 8
V7X_MXU_DIM = 256
V7X_VMEM_BYTES = 64 * 1024 * 1024
V7X_VMEM_REQUEST_CAP = 60000 * 1024

LRU_GROUP = V7X_MXU_DIM // LRU_BLOCK
N_LRU_GROUPS = D_MODEL // V7X_MXU_DIM
HALF = V7X_LANES // 2

V_CONV_W, V_CONV_B, V_BA, V_BX, V_LAM, V_BG, V_LN1G, V_LN1B, V_LN2G, V_LN2B = 0, 4, 5, 6, 7, 8, 11, 12, 13, 14
N_VEC = 16

NEG_INF = -1e30
LOG2_E = 1.4426950408889634


def _dot(a, b):
    return jnp.dot(a, b, preferred_element_type=F32)


def _dot_nt(a, b):
    return lax.dot_general(a, b, (((1,), (1,)), ((), ())), preferred_element_type=F32)


def _layer_norm(y, g, b):
    mu = jnp.mean(y, axis=-1, keepdims=True)
    yc = y - mu
    var = jnp.mean(yc * yc, axis=-1, keepdims=True)
    return yc * lax.rsqrt(var + LN_EPS) * g + b


def _gelu_tanh(x):
    return 0.5 * x * (1.0 + jnp.tanh(0.7978845608028654 * (x + 0.044715 * (x * x * x))))


def _softplus(x):
    return jnp.maximum(x, 0.0) + jnp.log1p(jnp.exp(-jnp.abs(x)))


def _lru_gates(xc, wri_ref, vec_ref, j):
    sl = slice(j * V7X_MXU_DIM, (j + 1) * V7X_MXU_DIM)
    xcj = xc[:, sl]
    ri = _dot(xcj.astype(BF16), wri_ref[j])
    r = jax.nn.sigmoid(ri[:, :V7X_MXU_DIM] + vec_ref[V_BA:V_BA + 1, sl])
    i = jax.nn.sigmoid(ri[:, V7X_MXU_DIM:] + vec_ref[V_BX:V_BX + 1, sl])
    log_a = (-LRU_C) * r * _softplus(-vec_ref[V_LAM:V_LAM + 1, sl])
    a = jnp.exp(log_a)
    u = jnp.sqrt(1.0 - a * a) * (i * xcj)
    return a, u


def _merge_tail(x, merged, wout_ref, vec_ref):
    y = _dot(merged.astype(BF16), wout_ref[...]) + ALPHA * x
    return _layer_norm(y, vec_ref[V_LN1G:V_LN1G + 1, :], vec_ref[V_LN1B:V_LN1B + 1, :])


def _ffn(h, wgu_ref, wd_ref, vec_ref):
    gu = _dot(h.astype(BF16), wgu_ref[...])
    gt, up = gu[:, :D_FF], gu[:, D_FF:]
    act = (gt * jax.nn.sigmoid(gt)) * up
    y = _dot(act.astype(BF16), wd_ref[...]) + ALPHA * h
    return _layer_norm(y, vec_ref[V_LN2G:V_LN2G + 1, :], vec_ref[V_LN2B:V_LN2B + 1, :])


def _mem_kv_kernel(m_ref, w_ref, k_ref, v_ref, kb_ref, vb_ref):
    kv = _dot(m_ref[...].astype(BF16), w_ref[...])
    k, v = kv[:, :D_MODEL], kv[:, D_MODEL:]
    k_ref[...] = k
    v_ref[...] = v
    kb_ref[...] = k.astype(BF16)
    vb_ref[...] = v.astype(BF16)


def _mem_kv(mem2d, w_mem_b, tm):
    rows = mem2d.shape[0]
    nt = rows // tm
    out_sds = jax.ShapeDtypeStruct((DEPTH, rows, D_MODEL), F32)
    out_b = jax.ShapeDtypeStruct((DEPTH, rows, D_MODEL), BF16)
    o_spec = pl.BlockSpec((None, tm, D_MODEL), lambda l, i: (l, i, 0))
    vmem = 2 * (tm * D_MODEL * 4 + D_MODEL * 2 * D_MODEL * 2 + 2 * tm * D_MODEL * 6) + 4 * tm * D_MODEL * 4
    return pl.pallas_call(
        _mem_kv_kernel,
        grid=(DEPTH, nt),
        in_specs=[pl.BlockSpec((tm, D_MODEL), lambda l, i: (i, 0)),
                  pl.BlockSpec((None, D_MODEL, 2 * D_MODEL), lambda l, i: (l, 0, 0))],
        out_specs=[o_spec, o_spec, o_spec, o_spec],
        out_shape=[out_sds, out_sds, out_b, out_b],
        compiler_params=pltpu.CompilerParams(dimension_semantics=("arbitrary", "arbitrary"),
                                             vmem_limit_bytes=min(vmem, V7X_VMEM_REQUEST_CAP)),
        name="mem_kv_proj",
    )(mem2d, w_mem_b)


def _prompt_layer_kernel(x_ref, mk_ref, mv_ref, w_in_ref, wri_ref, wbr_ref, wout_ref, vec_ref, sinks_ref,
                         h1_ref, pconv_ref, ph_ref, pwk_ref, pwv_ref,
                         xr_buf, a_s, u_s, h_s, h_carry, kb, vb, so_s, *, tm):
    t = pl.program_id(1)
    nt = pl.num_programs(1)

    @pl.when(t == 0)
    def _():
        xr_buf[0:V7X_SUBLANES, :] = jnp.zeros((V7X_SUBLANES, D_MODEL), F32)
        h_carry[...] = jnp.zeros((V7X_SUBLANES, V7X_LANES), F32)
        kb[0:WINDOW, :] = jnp.zeros((WINDOW, KV_W), BF16)
        vb[0:WINDOW, :] = jnp.zeros((WINDOW, KV_W), BF16)

    x = x_ref[...]
    xb = x.astype(BF16)

    def proj(c0, n):
        return _dot(xb, w_in_ref[:, c0:c0 + n])


    xr = proj(C_XR, D_MODEL)
    xr_buf[V7X_SUBLANES:V7X_SUBLANES + tm, :] = xr
    xc = vec_ref[V_CONV_B:V_CONV_B + 1, :] + vec_ref[V_CONV_W + 3:V_CONV_W + 4, :] * xr
    for k in range(CONV_W - 1):
        off = V7X_SUBLANES - (CONV_W - 1) + k
        xc = xc + vec_ref[V_CONV_W + k:V_CONV_W + k + 1, :] * xr_buf[off:off + tm, :]
    xr_buf[0:V7X_SUBLANES, :] = xr_buf[tm:tm + V7X_SUBLANES, :]
    pconv_ref[...] = xr_buf[0:V7X_SUBLANES, :]

    pitch = tm + V7X_SUBLANES
    lane_tiles = V7X_MXU_DIM // V7X_LANES

    def lru_group(j):
        a, u = _lru_gates(xc, wri_ref, vec_ref, j)
        for c in range(lane_tiles):
            r0 = (j * lane_tiles + c) * pitch
            a_s[r0:r0 + tm, :] = a[:, c * V7X_LANES:(c + 1) * V7X_LANES]
            u_s[r0:r0 + tm, :] = u[:, c * V7X_LANES:(c + 1) * V7X_LANES]

    lru_group(0)
    q = (proj(C_Q, D_MODEL) * (HEAD_DIM ** -0.5 * LOG2_E)).astype(BF16)
    lru_group(1)
    k = proj(C_K, KV_W)
    v = proj(C_V, KV_W)
    kb[WINDOW:WINDOW + tm, :] = k.astype(BF16)
    vb[WINDOW:WINDOW + tm, :] = v.astype(BF16)
    xq = (proj(C_XQ, D_MODEL) * (XHEAD_DIM ** -0.5 * LOG2_E)).astype(BF16)
    lru_group(2)
    gelu_y = _gelu_tanh(proj(C_YR, D_MODEL))
    lru_group(3)

    @pl.when(t == nt - 1)
    def _():
        pwk_ref[...] = k[tm - WINDOW:, :]
        pwv_ref[...] = v[tm - WINDOW:, :]

    h = h_carry[...]
    for r in range(tm):
        step = pl.ds(r, V7X_SUBLANES, stride=pitch)
        h = a_s[step, :] * h + u_s[step, :]
        h_s[step, :] = h
    h_carry[...] = h
    ph_ref[...] = h

    rows = GROUP * WINDOW
    qi = lax.broadcasted_iota(jnp.int32, (rows, 2 * WINDOW), 0) & (WINDOW - 1)
    kj = lax.broadcasted_iota(jnp.int32, (rows, 2 * WINDOW), 1)
    band = (kj > qi) & (kj <= qi + WINDOW)
    first_band = band & ((kj >= WINDOW) | (t > 0))
    lo_lanes = lax.broadcasted_iota(jnp.int32, (2 * WINDOW, V7X_LANES), 1) < HALF
    zero_b = jnp.zeros((2 * WINDOW, V7X_LANES), BF16)

    for n in range(tm // WINDOW):
        mask = first_band if n == 0 else band
        r0 = n * WINDOW
        combos = [(p, half) for p in range(N_KV // 2) for half in range(2)]
        qsts, vxs, scores = {}, {}, {}
        for p, half in combos:
            c0 = p * V7X_LANES
            if half == 0:
                qsts[p] = jnp.concatenate(
                    [q[r0:r0 + WINDOW, g * V7X_MXU_DIM + c0:g * V7X_MXU_DIM + c0 + V7X_LANES]
                     for g in range(GROUP)], axis=0)
            keep = lo_lanes if half == 0 else jnp.logical_not(lo_lanes)
            kx = jnp.where(keep, kb[r0:r0 + 2 * WINDOW, c0:c0 + V7X_LANES], zero_b)
            vxs[p, half] = jnp.where(keep, vb[r0:r0 + 2 * WINDOW, c0:c0 + V7X_LANES], zero_b)
            scores[p, half] = jnp.where(mask, _dot_nt(qsts[p], kx), NEG_INF)
        probs, recips = {}, {}
        for p, half in combos:
            head0 = (2 * p + half) * GROUP
            es, rs = [], []
            for g in range(GROUP):
                sink = sinks_ref[head0 + g] * LOG2_E
                sg = scores[p, half][g * WINDOW:(g + 1) * WINDOW, :]
                m = jnp.maximum(jnp.max(sg, axis=-1, keepdims=True), sink)
                e = jnp.exp2(sg - m)
                denom = jnp.sum(e, axis=-1, keepdims=True) + jnp.exp2(sink - m)
                es.append(e.astype(BF16))
                rs.append(1.0 / denom)
            probs[p, half] = jnp.concatenate(es, axis=0)
            recips[p, half] = jnp.concatenate(rs, axis=0)
        for p in range(N_KV // 2):
            c0 = p * V7X_LANES
            o_pair = (_dot(probs[p, 0], vxs[p, 0]) * recips[p, 0] + _dot(probs[p, 1], vxs[p, 1]) * recips[p, 1])
            for g in range(GROUP):
                so_s[r0:r0 + WINDOW, g * V7X_MXU_DIM + c0:g * V7X_MXU_DIM + c0 + V7X_LANES] = (
                    o_pair[g * WINDOW:(g + 1) * WINDOW, :].astype(BF16))

    kb[0:WINDOW, :] = kb[tm:tm + WINDOW, :]
    vb[0:WINDOW, :] = vb[tm:tm + WINDOW, :]

    head_sl = [slice(h * XHEAD_DIM, (h + 1) * XHEAD_DIM) for h in range(N_XHEADS)]
    scores = [_dot_nt(xq[:, sl], mk_ref[:, sl]) for sl in head_sl]
    probs, recips = [], []
    for s in scores:
        e = jnp.exp2(s - jnp.max(s, axis=-1, keepdims=True))
        probs.append(e.astype(BF16))
        recips.append(1.0 / jnp.sum(e, axis=-1, keepdims=True))
    c_out = jnp.concatenate(
        [(_dot(pr, mv_ref[:, sl]) * rc).astype(BF16) for pr, rc, sl in zip(probs, recips, head_sl)], axis=1)

    hs = jnp.concatenate([h_s[c * pitch:c * pitch + tm, :] for c in range(D_MODEL // V7X_LANES)], axis=1)
    merged = None
    for n, br in enumerate(((hs * gelu_y).astype(BF16), so_s[...], c_out)):
        gate = jax.nn.sigmoid(proj(C_G + n * D_MODEL, D_MODEL) + vec_ref[V_BG + n:V_BG + n + 1, :])
        part = gate * _dot(br, wbr_ref[n])
        merged = part if merged is None else merged + part

    h1_ref[...] = _merge_tail(x, merged, wout_ref, vec_ref)


def _prompt_layer(xp, mk_b, mv_b, w_in_b, wri_b, wbr_b, wout_b, vecs, sinks, layer, tm):
    batch, seq, _ = xp.shape
    nt = seq // tm
    const = dict(pipeline_mode=pl.Buffered(1))
    in_specs = [
        pl.BlockSpec((None, tm, D_MODEL), lambda b, t: (b, t, 0)),
        pl.BlockSpec((None, None, N_MEM, D_MODEL), lambda b, t: (layer, b, 0, 0)),
        pl.BlockSpec((None, None, N_MEM, D_MODEL), lambda b, t: (layer, b, 0, 0)),
        pl.BlockSpec((None, D_MODEL, IN_COLS), lambda b, t: (layer, 0, 0), **const),
        pl.BlockSpec((None, N_LRU_GROUPS, V7X_MXU_DIM, 2 * V7X_MXU_DIM), lambda b, t: (layer, 0, 0, 0), **const),
        pl.BlockSpec((None, N_BRANCH, D_MODEL, D_MODEL), lambda b, t: (layer, 0, 0, 0), **const),
        pl.BlockSpec((None, D_MODEL, D_MODEL), lambda b, t: (layer, 0, 0), **const),
        pl.BlockSpec((None, N_VEC, D_MODEL), lambda b, t: (layer, 0, 0), **const),
        pl.BlockSpec(memory_space=pltpu.SMEM),
    ]
    out_specs = [
        pl.BlockSpec((None, tm, D_MODEL), lambda b, t: (b, t, 0)),
        pl.BlockSpec((None, V7X_SUBLANES, D_MODEL), lambda b, t: (b, 0, 0)),
        pl.BlockSpec((None, V7X_SUBLANES, V7X_LANES), lambda b, t: (b, 0, 0)),
        pl.BlockSpec((None, WINDOW, KV_W), lambda b, t: (b, 0, 0)),
        pl.BlockSpec((None, WINDOW, KV_W), lambda b, t: (b, 0, 0)),
    ]
    out_shape = [
        jax.ShapeDtypeStruct((batch, seq, D_MODEL), F32),
        jax.ShapeDtypeStruct((batch, V7X_SUBLANES, D_MODEL), F32),
        jax.ShapeDtypeStruct((batch, V7X_SUBLANES, V7X_LANES), F32),
        jax.ShapeDtypeStruct((batch, WINDOW, KV_W), F32),
        jax.ShapeDtypeStruct((batch, WINDOW, KV_W), F32),
    ]
    assert (tm // V7X_SUBLANES + 1) % 2 == 1
    slab_rows = (D_MODEL // V7X_LANES) * (tm + V7X_SUBLANES)
    scratch = [
        pltpu.VMEM((tm + V7X_SUBLANES, D_MODEL), F32),
        pltpu.VMEM((slab_rows, V7X_LANES), F32),
        pltpu.VMEM((slab_rows, V7X_LANES), F32),
        pltpu.VMEM((slab_rows, V7X_LANES), F32),
        pltpu.VMEM((V7X_SUBLANES, V7X_LANES), F32),
        pltpu.VMEM((tm + WINDOW, KV_W), BF16),
        pltpu.VMEM((tm + WINDOW, KV_W), BF16),
        pltpu.VMEM((tm, D_MODEL), BF16),
    ]
    weights = (D_MODEL * IN_COLS + N_LRU_GROUPS * V7X_MXU_DIM * 2 * V7X_MXU_DIM
               + (N_BRANCH + 1) * D_MODEL * D_MODEL) * 2
    tile = tm * D_MODEL * 4
    vmem = weights + 4 * N_MEM * D_MODEL * 2 + 4 * tile + 5 * tile + 16 * tile
    return pl.pallas_call(
        functools.partial(_prompt_layer_kernel, tm=tm),
        grid=(batch, nt),
        in_specs=in_specs,
        out_specs=out_specs,
        out_shape=out_shape,
        scratch_shapes=scratch,
        compiler_params=pltpu.CompilerParams(dimension_semantics=("arbitrary", "arbitrary"),
                                             vmem_limit_bytes=min(vmem, V7X_VMEM_REQUEST_CAP)),
        name=f"prompt_layer_{layer}",
    )(xp, mk_b, mv_b, w_in_b, wri_b, wbr_b, wout_b, vecs, sinks)


def _ffn_kernel(h_ref, wgu_ref, wd_ref, vec_ref, o_ref):
    o_ref[...] = _ffn(h_ref[...], wgu_ref, wd_ref, vec_ref)


def _ffn_call(h2d, wgu_b, wd_b, vecs, layer, tm):
    rows = h2d.shape[0]
    const = dict(pipeline_mode=pl.Buffered(1))
    weights = 3 * D_MODEL * D_FF * 2
    tile = tm * D_MODEL * 4
    vmem = weights + 4 * tile + 2 * tm * 2 * D_FF * 4 + 2 * tile
    return pl.pallas_call(
        _ffn_kernel,
        grid=(rows // tm,),
        in_specs=[pl.BlockSpec((tm, D_MODEL), lambda i: (i, 0)),
                  pl.BlockSpec((None, D_MODEL, 2 * D_FF), lambda i: (layer, 0, 0), **const),
                  pl.BlockSpec((None, D_FF, D_MODEL), lambda i: (layer, 0, 0), **const),
                  pl.BlockSpec((None, N_VEC, D_MODEL), lambda i: (layer, 0, 0), **const)],
        out_specs=pl.BlockSpec((tm, D_MODEL), lambda i: (i, 0)),
        out_shape=jax.ShapeDtypeStruct((rows, D_MODEL), F32),
        compiler_params=pltpu.CompilerParams(dimension_semantics=("arbitrary",),
                                             vmem_limit_bytes=min(vmem, V7X_VMEM_REQUEST_CAP)),
        name=f"prompt_ffn_{layer}",
    )(h2d, wgu_b, wd_b, vecs)


def _decode_proj_kernel(x_ref, sc_ref, h0_ref, w_in_ref, wri_ref, vec_ref,
                        aout_ref, q_ref, k_ref, v_ref, xq_ref, gate_ref, sconv_ref, sh_ref):
    x = x_ref[...]
    xb = x.astype(BF16)

    def proj(c0, n):
        return _dot(xb, w_in_ref[:, c0:c0 + n])

    xr = proj(C_XR, D_MODEL)
    xc = vec_ref[V_CONV_B:V_CONV_B + 1, :] + vec_ref[V_CONV_W + 3:V_CONV_W + 4, :] * xr
    for k in range(CONV_W - 1):
        xc = xc + vec_ref[V_CONV_W + k:V_CONV_W + k + 1, :] * sc_ref[:, k * D_MODEL:(k + 1) * D_MODEL]
    sconv_ref[:, 0:(CONV_W - 2) * D_MODEL] = sc_ref[:, D_MODEL:(CONV_W - 1) * D_MODEL]
    sconv_ref[:, (CONV_W - 2) * D_MODEL:] = xr

    gelu_y = _gelu_tanh(proj(C_YR, D_MODEL))
    for j in range(N_LRU_GROUPS):
        sl = slice(j * V7X_MXU_DIM, (j + 1) * V7X_MXU_DIM)
        a, u = _lru_gates(xc, wri_ref, vec_ref, j)
        h = a * h0_ref[:, sl] + u
        sh_ref[:, sl] = h
        aout_ref[:, sl] = h * gelu_y[:, sl]

    q_ref[...] = proj(C_Q, D_MODEL)
    k_ref[...] = proj(C_K, KV_W)
    v_ref[...] = proj(C_V, KV_W)
    xq_ref[...] = proj(C_XQ, D_MODEL)
    for n in range(N_BRANCH):
        sl = slice(n * D_MODEL, (n + 1) * D_MODEL)
        gate_ref[:, sl] = jax.nn.sigmoid(proj(C_G + n * D_MODEL, D_MODEL) + vec_ref[V_BG + n:V_BG + n + 1, :])


def _decode_proj(xs, sc, h0, w_in_b, wri_b, vecs, layer):
    b = xs.shape[0]
    sds = lambda n: jax.ShapeDtypeStruct((b, n), F32)
    full = lambda n: pl.BlockSpec((b, n), lambda i: (0, 0))
    weights = (D_MODEL * IN_COLS + N_LRU_GROUPS * V7X_MXU_DIM * 2 * V7X_MXU_DIM) * 2
    vmem = weights + 2 * 16 * b * D_MODEL * 4 + 8 * b * D_MODEL * 4
    return pl.pallas_call(
        _decode_proj_kernel,
        grid=(1,),
        in_specs=[full(D_MODEL), full((CONV_W - 1) * D_MODEL), full(D_MODEL),
                  pl.BlockSpec((None, D_MODEL, IN_COLS), lambda i: (layer, 0, 0)),
                  pl.BlockSpec((None, N_LRU_GROUPS, V7X_MXU_DIM, 2 * V7X_MXU_DIM), lambda i: (layer, 0, 0, 0)),
                  pl.BlockSpec((None, N_VEC, D_MODEL), lambda i: (layer, 0, 0))],
        out_specs=[full(D_MODEL), full(D_MODEL), full(KV_W), full(KV_W), full(D_MODEL),
                   full(N_BRANCH * D_MODEL), full((CONV_W - 1) * D_MODEL), full(D_MODEL)],
        out_shape=[sds(D_MODEL), sds(D_MODEL), sds(KV_W), sds(KV_W), sds(D_MODEL),
                   sds(N_BRANCH * D_MODEL), sds((CONV_W - 1) * D_MODEL), sds(D_MODEL)],
        compiler_params=pltpu.CompilerParams(dimension_semantics=("arbitrary",),
                                             vmem_limit_bytes=min(vmem, V7X_VMEM_REQUEST_CAP)),
        name=f"decode_proj_{layer}",
    )(xs, sc, h0, w_in_b, wri_b, vecs)


def _decode_attn_kernel(q16_ref, knew_ref, vnew_ref, wk_ref, wv_ref, xq_ref, mk_ref, mv_ref, sink_ref,
                        owk_ref, owv_ref, so_ref, co_ref, *, bt):
    row = lax.broadcasted_iota(jnp.int32, (WINDOW, KV_W), 0)
    head_row = lax.broadcasted_iota(jnp.int32, (N_HEADS, KV_W), 0)
    head_lane = lax.broadcasted_iota(jnp.int32, (N_HEADS, KV_W), 1)
    own = (head_lane >> (HEAD_DIM.bit_length() - 1)) == (head_row >> (GROUP.bit_length() - 1))
    sink = sink_ref[...]
    for i in range(bt):
        kwin = jnp.where(row == WINDOW - 1, knew_ref[i], pltpu.roll(wk_ref[i], WINDOW - 1, 0))
        vwin = jnp.where(row == WINDOW - 1, vnew_ref[i], pltpu.roll(wv_ref[i], WINDOW - 1, 0))
        owk_ref[i] = kwin
        owv_ref[i] = vwin
        q = jnp.where(own, q16_ref[i] * (HEAD_DIM ** -0.5), 0.0).astype(BF16)
        s = _dot_nt(q, kwin.astype(BF16))
        m = jnp.maximum(jnp.max(s, axis=-1, keepdims=True), sink)
        e = jnp.exp(s - m)
        denom = jnp.sum(e, axis=-1, keepdims=True) + jnp.exp(sink - m)
        o = jnp.where(own, _dot(e.astype(BF16), vwin.astype(BF16)) / denom, 0.0)
        acc = o[0:GROUP, :]
        for kk in range(1, N_KV):
            acc = acc + o[kk * GROUP:(kk + 1) * GROUP, :]
        so_ref[i] = acc

        sc = jnp.sum(mk_ref[i] * (xq_ref[i] * (XHEAD_DIM ** -0.5))[None], axis=-1, keepdims=True)
        ex = jnp.exp(sc - jnp.max(sc, axis=0, keepdims=True))
        pr = ex / jnp.sum(ex, axis=0, keepdims=True)
        co_ref[i] = jnp.sum(pr * mv_ref[i], axis=0)


def _decode_attn(q16, knew, vnew, win_k, win_v, xq, mem_k, mem_v, sink_col, layer, bt):
    b = q16.shape[0]
    cache = pl.BlockSpec((None, bt, WINDOW, KV_W), lambda i: (layer, i, 0, 0))
    memsp = pl.BlockSpec((None, bt, N_MEM, N_XHEADS, XHEAD_DIM), lambda i: (layer, i, 0, 0, 0))
    mem_block = bt * N_MEM * V7X_SUBLANES * XHEAD_DIM * 4
    vmem = 2 * (2 * mem_block + bt * (4 * WINDOW * KV_W + 8 * D_MODEL) * 4) + 2 * mem_block
    return pl.pallas_call(
        functools.partial(_decode_attn_kernel, bt=bt),
        grid=(b // bt,),
        in_specs=[pl.BlockSpec((bt, N_HEADS, KV_W), lambda i: (i, 0, 0)),
                  pl.BlockSpec((bt, 1, KV_W), lambda i: (i, 0, 0)),
                  pl.BlockSpec((bt, 1, KV_W), lambda i: (i, 0, 0)),
                  cache, cache,
                  pl.BlockSpec((bt, N_XHEADS, XHEAD_DIM), lambda i: (i, 0, 0)),
                  memsp, memsp,
                  pl.BlockSpec((None, N_HEADS, 1), lambda i: (layer, 0, 0))],
        out_specs=[pl.BlockSpec((bt, WINDOW, KV_W), lambda i: (i, 0, 0)),
                   pl.BlockSpec((bt, WINDOW, KV_W), lambda i: (i, 0, 0)),
                   pl.BlockSpec((bt, GROUP, KV_W), lambda i: (i, 0, 0)),
                   pl.BlockSpec((bt, N_XHEADS, XHEAD_DIM), lambda i: (i, 0, 0))],
        out_shape=[jax.ShapeDtypeStruct((b, WINDOW, KV_W), F32),
                   jax.ShapeDtypeStruct((b, WINDOW, KV_W), F32),
                   jax.ShapeDtypeStruct((b, GROUP, KV_W), F32),
                   jax.ShapeDtypeStruct((b, N_XHEADS, XHEAD_DIM), F32)],
        compiler_params=pltpu.CompilerParams(dimension_semantics=("arbitrary",),
                                             vmem_limit_bytes=min(vmem, V7X_VMEM_REQUEST_CAP)),
        name=f"decode_attn_{layer}",
    )(q16, knew, vnew, win_k, win_v, xq, mem_k, mem_v, sink_col)


def _decode_tail_kernel(x_ref, aout_ref, so_ref, co_ref, gate_ref, wbr_ref, wout_ref, wgu_ref, wd_ref, vec_ref,
                        o_ref):
    merged = None
    for n, br_ref in enumerate((aout_ref, so_ref, co_ref)):
        part = gate_ref[:, n * D_MODEL:(n + 1) * D_MODEL] * _dot(br_ref[...].astype(BF16), wbr_ref[n])
        merged = part if merged is None else merged + part
    h = _merge_tail(x_ref[...], merged, wout_ref, vec_ref)
    o_ref[...] = _ffn(h, wgu_ref, wd_ref, vec_ref)


def _decode_tail(xs, aout, so, co, gates, wbr_b, wout_b, wgu_b, wd_b, vecs, layer):
    b = xs.shape[0]
    full = lambda n: pl.BlockSpec((b, n), lambda i: (0, 0))
    weights = ((N_BRANCH + 1) * D_MODEL * D_MODEL + 3 * D_MODEL * D_FF) * 2
    vmem = weights + 2 * 8 * b * D_MODEL * 4 + 3 * b * 2 * D_FF * 4 + 8 * b * D_MODEL * 4
    return pl.pallas_call(
        _decode_tail_kernel,
        grid=(1,),
        in_specs=[full(D_MODEL), full(D_MODEL), full(D_MODEL), full(D_MODEL), full(N_BRANCH * D_MODEL),
                  pl.BlockSpec((None, N_BRANCH, D_MODEL, D_MODEL), lambda i: (layer, 0, 0, 0)),
                  pl.BlockSpec((None, D_MODEL, D_MODEL), lambda i: (layer, 0, 0)),
                  pl.BlockSpec((None, D_MODEL, 2 * D_FF), lambda i: (layer, 0, 0)),
                  pl.BlockSpec((None, D_FF, D_MODEL), lambda i: (layer, 0, 0)),
                  pl.BlockSpec((None, N_VEC, D_MODEL), lambda i: (layer, 0, 0))],
        out_specs=full(D_MODEL),
        out_shape=jax.ShapeDtypeStruct((b, D_MODEL), F32),
        compiler_params=pltpu.CompilerParams(dimension_semantics=("arbitrary",),
                                             vmem_limit_bytes=min(vmem, V7X_VMEM_REQUEST_CAP)),
        name=f"decode_tail_{layer}",
    )(xs, aout, so, co, gates, wbr_b, wout_b, wgu_b, wd_b, vecs)


def _head_major_to_member_major(w, axis):
    shape = w.shape
    w = w.reshape(shape[:axis] + (N_KV, GROUP, HEAD_DIM) + shape[axis + 1:])
    w = jnp.swapaxes(w, axis, axis + 1)
    return w.reshape(shape)


def _block_diag_groups(w):
    depth = w.shape[0]
    w = w.reshape(depth, N_LRU_GROUPS, LRU_GROUP, LRU_BLOCK, LRU_BLOCK)
    eye = jnp.eye(LRU_GROUP, dtype=w.dtype)
    bd = w[:, :, :, :, None, :] * eye[None, None, :, None, :, None]
    return bd.reshape(depth, N_LRU_GROUPS, V7X_MXU_DIM, V7X_MXU_DIM)


def kernel(x_prompt, x_sample, state_conv, state_h, cache_win_k, cache_win_v, cache_mem_k, cache_mem_v, mem_prompt, w_mem_kv, w_in, b_gates, conv_w, conv_b, lru_wa, lru_ba, lru_wx, lru_bx, lru_lambda, sinks, w_branch, w_out, ln1_g, ln1_b, w_gate_up, w_down, ln2_g, ln2_b):
    bp, seq, _ = x_prompt.shape
    bs = x_sample.shape[0]

    w_in_b = jnp.concatenate(
        [w_in[..., :C_Q], _head_major_to_member_major(w_in[..., C_Q:C_K], 2), w_in[..., C_K:]], axis=-1).astype(BF16)
    wbr_b = jnp.stack(
        [w_branch[:, 0], _head_major_to_member_major(w_branch[:, 1], 1), w_branch[:, 2]], axis=1).astype(BF16)
    wri_b = jnp.concatenate([_block_diag_groups(lru_wa), _block_diag_groups(lru_wx)], axis=-1).astype(BF16)
    wout_b = w_out.astype(BF16)
    wgu_b = w_gate_up.astype(BF16)
    wd_b = w_down.astype(BF16)
    wmem_b = w_mem_kv.astype(BF16)
    row = lambda p: p[:, None, :]
    vecs = jnp.concatenate(
        [conv_w, row(conv_b), row(lru_ba), row(lru_bx), row(lru_lambda), b_gates, row(ln1_g), row(ln1_b),
         row(ln2_g), row(ln2_b), jnp.zeros((DEPTH, N_VEC - 15, D_MODEL), F32)], axis=1)
    sinks_flat = sinks.reshape(DEPTH * N_HEADS)
    sink_col = sinks[:, :, None]

    mk, mv, mk_b, mv_b = _mem_kv(mem_prompt.reshape(bp * N_MEM, D_MODEL), wmem_b, tm=512)
    mk_b = mk_b.reshape(DEPTH, bp, N_MEM, D_MODEL)
    mv_b = mv_b.reshape(DEPTH, bp, N_MEM, D_MODEL)

    win_k = cache_win_k.reshape(DEPTH, bs, WINDOW, KV_W)
    win_v = cache_win_v.reshape(DEPTH, bs, WINDOW, KV_W)

    xp = x_prompt
    xs = x_sample.reshape(bs, D_MODEL)
    p_conv, p_h, p_wk, p_wv = [], [], [], []
    s_conv, s_h, s_wk, s_wv = [], [], [], []
    for l in range(DEPTH):
        h1, pc8, ph, pwk, pwv = _prompt_layer(xp, mk_b, mv_b, w_in_b, wri_b, wbr_b, wout_b, vecs,
                                              sinks_flat[l * N_HEADS:(l + 1) * N_HEADS], l, tm=256)
        xp = _ffn_call(h1.reshape(bp * seq, D_MODEL), wgu_b, wd_b, vecs, l, tm=512).reshape(bp, seq, D_MODEL)
        p_conv.append(pc8[:, V7X_SUBLANES - (CONV_W - 1):, :])
        p_h.append(ph.reshape(bp, D_MODEL))
        p_wk.append(pwk.reshape(bp, WINDOW, N_KV, HEAD_DIM))
        p_wv.append(pwv.reshape(bp, WINDOW, N_KV, HEAD_DIM))

        aout, q, knew, vnew, xq, gates, sconv, sh = _decode_proj(
            xs, state_conv[l].reshape(bs, (CONV_W - 1) * D_MODEL), state_h[l], w_in_b, wri_b, vecs, l)
        q16 = jnp.tile(q.reshape(bs, GROUP, KV_W), (1, N_KV, 1))
        owk, owv, so, co = _decode_attn(q16, knew.reshape(bs, 1, KV_W), vnew.reshape(bs, 1, KV_W), win_k, win_v,
                                        xq.reshape(bs, N_XHEADS, XHEAD_DIM), cache_mem_k, cache_mem_v, sink_col, l, bt=4)
        xs = _decode_tail(xs, aout, so.reshape(bs, D_MODEL), co.reshape(bs, D_MODEL), gates,
                          wbr_b, wout_b, wgu_b, wd_b, vecs, l)
        s_conv.append(sconv.reshape(bs, CONV_W - 1, D_MODEL))
        s_h.append(sh)
        s_wk.append(owk.reshape(bs, WINDOW, N_KV, HEAD_DIM))
        s_wv.append(owv.reshape(bs, WINDOW, N_KV, HEAD_DIM))

    return (xp, xs.reshape(bs, 1, D_MODEL),
            jnp.stack(p_conv), jnp.stack(p_h), jnp.stack(p_wk), jnp.stack(p_wv),
            mk.reshape(DEPTH, bp, N_MEM, N_XHEADS, XHEAD_DIM), mv.reshape(DEPTH, bp, N_MEM, N_XHEADS, XHEAD_DIM),
            jnp.stack(s_conv), jnp.stack(s_h), jnp.stack(s_wk), jnp.stack(s_wv))
```

```python
import functools

import jax
import jax.numpy as jnp
from jax import lax
from jax.experimental import pallas as pl
from jax.experimental.pallas import tpu as pltpu

F32 = jnp.float32
BF16 = jnp.bfloat16

D_MODEL = 1024
DEPTH = 2
LRU_BLOCKS = 16
LRU_BLOCK = D_MODEL // LRU_BLOCKS
CONV_W = 4
LRU_C = 8.0
N_HEADS = 16
N_KV = 4
HEAD_DIM = D_MODEL // N_HEADS
GROUP = N_HEADS // N_KV
WINDOW = 128
KV_W = N_KV * HEAD_DIM
N_MEM = 256
N_XHEADS = 4
XHEAD_DIM = D_MODEL // N_XHEADS
N_BRANCH = 3
D_FF = -(-8 * D_MODEL // (3 * 256)) * 256
LN_EPS = 1e-5
ALPHA = (2 * DEPTH) ** 0.25
IN_COLS = 3 * D_MODEL + 2 * KV_W + D_MODEL + N_BRANCH * D_MODEL
C_XR, C_YR, C_Q, C_K, C_V, C_XQ, C_G = 0, 1024, 2048, 3072, 3328, 3584, 4608

V7X_LANES = 128
V7X_SUBLANES = 8
V7X_MXU_DIM = 256
V7X_VMEM_BYTES = 64 * 1024 * 1024
V7X_VMEM_REQUEST_CAP = 60000 * 1024

LRU_GROUP = V7X_MXU_DIM // LRU_BLOCK
N_LRU_GROUPS = D_MODEL // V7X_MXU_DIM
HALF = V7X_LANES // 2

V_CONV_W, V_CONV_B, V_BA, V_BX, V_LAM, V_BG, V_LN1G, V_LN1B, V_LN2G, V_LN2B = 0, 4, 5, 6, 7, 8, 11, 12, 13, 14
N_VEC = 16

NEG_INF = -1e30
LOG2_E = 1.4426950408889634


def _dot(a, b):
    return jnp.dot(a, b, preferred_element_type=F32)


def _dot_nt(a, b):
    return lax.dot_general(a, b, (((1,), (1,)), ((), ())), preferred_element_type=F32)


def _layer_norm(y, g, b):
    mu = jnp.mean(y, axis=-1, keepdims=True)
    yc = y - mu
    var = jnp.mean(yc * yc, axis=-1, keepdims=True)
    return yc * lax.rsqrt(var + LN_EPS) * g + b


def _gelu_tanh(x):
    return 0.5 * x * (1.0 + jnp.tanh(0.7978845608028654 * (x + 0.044715 * (x * x * x))))


def _sigmoid(x):
    return 0.5 * jnp.tanh(0.5 * x) + 0.5


def _softplus(x):
    return jnp.maximum(x, 0.0) + jnp.log1p(jnp.exp(-jnp.abs(x)))


def _lru_gates(xc, wri_ref, vec_ref, j):
    sl = slice(j * V7X_MXU_DIM, (j + 1) * V7X_MXU_DIM)
    xcj = xc[:, sl]
    ri = _dot(xcj.astype(BF16), wri_ref[j])
    r = _sigmoid(ri[:, :V7X_MXU_DIM] + vec_ref[V_BA:V_BA + 1, sl])
    i = _sigmoid(ri[:, V7X_MXU_DIM:] + vec_ref[V_BX:V_BX + 1, sl])
    log_a = (-LRU_C) * r * _softplus(-vec_ref[V_LAM:V_LAM + 1, sl])
    a = jnp.exp(log_a)
    u = jnp.sqrt(1.0 - a * a) * (i * xcj)
    return a, u


def _merge_tail(x, merged, wout_ref, vec_ref):
    y = _dot(merged.astype(BF16), wout_ref[...]) + ALPHA * x
    return _layer_norm(y, vec_ref[V_LN1G:V_LN1G + 1, :], vec_ref[V_LN1B:V_LN1B + 1, :])


def _ffn(h, wgu_ref, wd_ref, vec_ref):
    gu = _dot(h.astype(BF16), wgu_ref[...])
    gt, up = gu[:, :D_FF], gu[:, D_FF:]
    act = (gt * _sigmoid(gt)) * up
    y = _dot(act.astype(BF16), wd_ref[...]) + ALPHA * h
    return _layer_norm(y, vec_ref[V_LN2G:V_LN2G + 1, :], vec_ref[V_LN2B:V_LN2B + 1, :])


def _mem_kv_kernel(m_ref, w_ref, k_ref, v_ref, kb_ref, vb_ref):
    kv = _dot(m_ref[...].astype(BF16), w_ref[...])
    k, v = kv[:, :D_MODEL], kv[:, D_MODEL:]
    k_ref[...] = k
    v_ref[...] = v
    kb_ref[...] = k.astype(BF16)
    vb_ref[...] = v.astype(BF16)


def _mem_kv(mem2d, w_mem_b, tm):
    rows = mem2d.shape[0]
    nt = rows // tm
    out_sds = jax.ShapeDtypeStruct((DEPTH, rows, D_MODEL), F32)
    out_b = jax.ShapeDtypeStruct((DEPTH, rows, D_MODEL), BF16)
    o_spec = pl.BlockSpec((None, tm, D_MODEL), lambda l, i: (l, i, 0))
    vmem = 2 * (tm * D_MODEL * 4 + D_MODEL * 2 * D_MODEL * 2 + 2 * tm * D_MODEL * 6) + 4 * tm * D_MODEL * 4
    return pl.pallas_call(
        _mem_kv_kernel,
        grid=(DEPTH, nt),
        in_specs=[pl.BlockSpec((tm, D_MODEL), lambda l, i: (i, 0)),
                  pl.BlockSpec((None, D_MODEL, 2 * D_MODEL), lambda l, i: (l, 0, 0))],
        out_specs=[o_spec, o_spec, o_spec, o_spec],
        out_shape=[out_sds, out_sds, out_b, out_b],
        compiler_params=pltpu.CompilerParams(dimension_semantics=("arbitrary", "arbitrary"),
                                             vmem_limit_bytes=min(vmem, V7X_VMEM_REQUEST_CAP)),
        name="mem_kv_proj",
    )(mem2d, w_mem_b)


def _prompt_layer_kernel(x_ref, mk_ref, mv_ref, w_in_ref, wri_ref, wbr_ref, wout_ref, vec_ref, sinks_ref,
                         y1_ref, pconv_ref, ph_ref, pwk_ref, pwv_ref,
                         xr_buf, a_s, u_s, h_s, h_carry, kb, vb, so_s, *, tm):
    t = pl.program_id(1)
    nt = pl.num_programs(1)

    @pl.when(t == 0)
    def _():
        xr_buf[0:V7X_SUBLANES, :] = jnp.zeros((V7X_SUBLANES, D_MODEL), F32)
        h_carry[...] = jnp.zeros((V7X_SUBLANES, V7X_LANES), F32)
        kb[0:WINDOW, :] = jnp.zeros((WINDOW, KV_W), BF16)
        vb[0:WINDOW, :] = jnp.zeros((WINDOW, KV_W), BF16)

    x = x_ref[...]
    xb = x.astype(BF16)

    def proj(c0, n):
        return _dot(xb, w_in_ref[:, c0:c0 + n])


    xr = proj(C_XR, D_MODEL)
    xr_buf[V7X_SUBLANES:V7X_SUBLANES + tm, :] = xr
    xc = vec_ref[V_CONV_B:V_CONV_B + 1, :] + vec_ref[V_CONV_W + 3:V_CONV_W + 4, :] * xr
    for k in range(CONV_W - 1):
        off = V7X_SUBLANES - (CONV_W - 1) + k
        xc = xc + vec_ref[V_CONV_W + k:V_CONV_W + k + 1, :] * xr_buf[off:off + tm, :]
    xr_buf[0:V7X_SUBLANES, :] = xr_buf[tm:tm + V7X_SUBLANES, :]
    pconv_ref[...] = xr_buf[0:V7X_SUBLANES, :]

    pitch = tm + V7X_SUBLANES
    lane_tiles = V7X_MXU_DIM // V7X_LANES

    def lru_group(j):
        a, u = _lru_gates(xc, wri_ref, vec_ref, j)
        for c in range(lane_tiles):
            r0 = (j * lane_tiles + c) * pitch
            a_s[r0:r0 + tm, :] = a[:, c * V7X_LANES:(c + 1) * V7X_LANES]
            u_s[r0:r0 + tm, :] = u[:, c * V7X_LANES:(c + 1) * V7X_LANES]

    lru_group(0)
    q = (proj(C_Q, D_MODEL) * (HEAD_DIM ** -0.5 * LOG2_E)).astype(BF16)
    lru_group(1)
    k = proj(C_K, KV_W)
    v = proj(C_V, KV_W)
    kb[WINDOW:WINDOW + tm, :] = k.astype(BF16)
    vb[WINDOW:WINDOW + tm, :] = v.astype(BF16)
    xq = (proj(C_XQ, D_MODEL) * (XHEAD_DIM ** -0.5 * LOG2_E)).astype(BF16)
    lru_group(2)
    gelu_y = _gelu_tanh(proj(C_YR, D_MODEL))
    lru_group(3)

    @pl.when(t == nt - 1)
    def _():
        pwk_ref[...] = k[tm - WINDOW:, :]
        pwv_ref[...] = v[tm - WINDOW:, :]

    rows = GROUP * WINDOW
    qi = lax.broadcasted_iota(jnp.int32, (rows, 2 * WINDOW), 0) & (WINDOW - 1)
    kj = lax.broadcasted_iota(jnp.int32, (rows, 2 * WINDOW), 1)
    band = (kj > qi) & (kj <= qi + WINDOW)
    first_band = band & ((kj >= WINDOW) | (t > 0))
    lo_lanes = lax.broadcasted_iota(jnp.int32, (2 * WINDOW, V7X_LANES), 1) < HALF
    zero_b = jnp.zeros((2 * WINDOW, V7X_LANES), BF16)

    for n in range(tm // WINDOW):
        mask = first_band if n == 0 else band
        r0 = n * WINDOW
        combos = [(p, half) for p in range(N_KV // 2) for half in range(2)]
        qsts, vxs, scores = {}, {}, {}
        for p, half in combos:
            c0 = p * V7X_LANES
            if half == 0:
                qsts[p] = jnp.concatenate(
                    [q[r0:r0 + WINDOW, g * V7X_MXU_DIM + c0:g * V7X_MXU_DIM + c0 + V7X_LANES]
                     for g in range(GROUP)], axis=0)
            keep = lo_lanes if half == 0 else jnp.logical_not(lo_lanes)
            kx = jnp.where(keep, kb[r0:r0 + 2 * WINDOW, c0:c0 + V7X_LANES], zero_b)
            vxs[p, half] = jnp.where(keep, vb[r0:r0 + 2 * WINDOW, c0:c0 + V7X_LANES], zero_b)
            scores[p, half] = jnp.where(mask, _dot_nt(qsts[p], kx), NEG_INF)
        probs, recips = {}, {}
        for p, half in combos:
            head0 = (2 * p + half) * GROUP
            es, rs = [], []
            for g in range(GROUP):
                sink = sinks_ref[head0 + g] * LOG2_E
                sg = scores[p, half][g * WINDOW:(g + 1) * WINDOW, :]
                m = jnp.maximum(jnp.max(sg, axis=-1, keepdims=True), sink)
                e = jnp.exp2(sg - m)
                denom = jnp.sum(e, axis=-1, keepdims=True) + jnp.exp2(sink - m)
                es.append(e.astype(BF16))
                rs.append(1.0 / denom)
            probs[p, half] = jnp.concatenate(es, axis=0)
            recips[p, half] = jnp.concatenate(rs, axis=0)
        for p in range(N_KV // 2):
            c0 = p * V7X_LANES
            o_pair = (_dot(probs[p, 0], vxs[p, 0]) * recips[p, 0] + _dot(probs[p, 1], vxs[p, 1]) * recips[p, 1])
            for g in range(GROUP):
                so_s[r0:r0 + WINDOW, g * V7X_MXU_DIM + c0:g * V7X_MXU_DIM + c0 + V7X_LANES] = (
                    o_pair[g * WINDOW:(g + 1) * WINDOW, :].astype(BF16))

    kb[0:WINDOW, :] = kb[tm:tm + WINDOW, :]
    vb[0:WINDOW, :] = vb[tm:tm + WINDOW, :]

    head_sl = [slice(h * XHEAD_DIM, (h + 1) * XHEAD_DIM) for h in range(N_XHEADS)]
    scores = [_dot_nt(xq[:, sl], mk_ref[:, sl]) for sl in head_sl]
    probs, recips = [], []
    for s in scores:
        e = jnp.exp2(s - jnp.max(s, axis=-1, keepdims=True))
        probs.append(e.astype(BF16))
        recips.append(1.0 / jnp.sum(e, axis=-1, keepdims=True))
    c_out = jnp.concatenate(
        [(_dot(pr, mv_ref[:, sl]) * rc).astype(BF16) for pr, rc, sl in zip(probs, recips, head_sl)], axis=1)

    h = h_carry[...]
    for r in range(tm):
        step = pl.ds(r, V7X_SUBLANES, stride=pitch)
        h = a_s[step, :] * h + u_s[step, :]
        h_s[step, :] = h
    h_carry[...] = h
    ph_ref[...] = h

    def gated(n, br):
        gate = _sigmoid(proj(C_G + n * D_MODEL, D_MODEL) + vec_ref[V_BG + n:V_BG + n + 1, :])
        return gate * _dot(br, wbr_ref[n])

    part_b = gated(1, so_s[...])
    part_c = gated(2, c_out)
    hs = jnp.concatenate([h_s[c * pitch:c * pitch + tm, :] for c in range(D_MODEL // V7X_LANES)], axis=1)
    merged = (gated(0, (hs * gelu_y).astype(BF16)) + part_b) + part_c

    y1_ref[...] = _dot(merged.astype(BF16), wout_ref[...]) + ALPHA * x


def _prompt_layer(xp, mk_b, mv_b, w_in_b, wri_b, wbr_b, wout_b, vecs, sinks, layer, tm):
    batch, seq, _ = xp.shape
    nt = seq // tm
    const = dict(pipeline_mode=pl.Buffered(1))
    in_specs = [
        pl.BlockSpec((None, tm, D_MODEL), lambda b, t: (b, t, 0)),
        pl.BlockSpec((None, None, N_MEM, D_MODEL), lambda b, t: (layer, b, 0, 0)),
        pl.BlockSpec((None, None, N_MEM, D_MODEL), lambda b, t: (layer, b, 0, 0)),
        pl.BlockSpec((None, D_MODEL, IN_COLS), lambda b, t: (layer, 0, 0), **const),
        pl.BlockSpec((None, N_LRU_GROUPS, V7X_MXU_DIM, 2 * V7X_MXU_DIM), lambda b, t: (layer, 0, 0, 0), **const),
        pl.BlockSpec((None, N_BRANCH, D_MODEL, D_MODEL), lambda b, t: (layer, 0, 0, 0), **const),
        pl.BlockSpec((None, D_MODEL, D_MODEL), lambda b, t: (layer, 0, 0), **const),
        pl.BlockSpec((None, N_VEC, D_MODEL), lambda b, t: (layer, 0, 0), **const),
        pl.BlockSpec(memory_space=pltpu.SMEM),
    ]
    out_specs = [
        pl.BlockSpec((None, tm, D_MODEL), lambda b, t: (b, t, 0)),
        pl.BlockSpec((None, V7X_SUBLANES, D_MODEL), lambda b, t: (b, 0, 0)),
        pl.BlockSpec((None, V7X_SUBLANES, V7X_LANES), lambda b, t: (b, 0, 0)),
        pl.BlockSpec((None, WINDOW, KV_W), lambda b, t: (b, 0, 0)),
        pl.BlockSpec((None, WINDOW, KV_W), lambda b, t: (b, 0, 0)),
    ]
    out_shape = [
        jax.ShapeDtypeStruct((batch, seq, D_MODEL), F32),
        jax.ShapeDtypeStruct((batch, V7X_SUBLANES, D_MODEL), F32),
        jax.ShapeDtypeStruct((batch, V7X_SUBLANES, V7X_LANES), F32),
        jax.ShapeDtypeStruct((batch, WINDOW, KV_W), F32),
        jax.ShapeDtypeStruct((batch, WINDOW, KV_W), F32),
    ]
    assert (tm // V7X_SUBLANES + 1) % 2 == 1
    slab_rows = (D_MODEL // V7X_LANES) * (tm + V7X_SUBLANES)
    scratch = [
        pltpu.VMEM((tm + V7X_SUBLANES, D_MODEL), F32),
        pltpu.VMEM((slab_rows, V7X_LANES), F32),
        pltpu.VMEM((slab_rows, V7X_LANES), F32),
        pltpu.VMEM((slab_rows, V7X_LANES), F32),
        pltpu.VMEM((V7X_SUBLANES, V7X_LANES), F32),
        pltpu.VMEM((tm + WINDOW, KV_W), BF16),
        pltpu.VMEM((tm + WINDOW, KV_W), BF16),
        pltpu.VMEM((tm, D_MODEL), BF16),
    ]
    weights = (D_MODEL * IN_COLS + N_LRU_GROUPS * V7X_MXU_DIM * 2 * V7X_MXU_DIM
               + (N_BRANCH + 1) * D_MODEL * D_MODEL) * 2
    tile = tm * D_MODEL * 4
    vmem = weights + 4 * N_MEM * D_MODEL * 2 + 4 * tile + 5 * tile + 16 * tile
    return pl.pallas_call(
        functools.partial(_prompt_layer_kernel, tm=tm),
        grid=(batch, nt),
        in_specs=in_specs,
        out_specs=out_specs,
        out_shape=out_shape,
        scratch_shapes=scratch,
        compiler_params=pltpu.CompilerParams(dimension_semantics=("arbitrary", "arbitrary"),
                                             vmem_limit_bytes=min(vmem, V7X_VMEM_REQUEST_CAP)),
        name=f"prompt_layer_{layer}",
    )(xp, mk_b, mv_b, w_in_b, wri_b, wbr_b, wout_b, vecs, sinks)


def _ffn_kernel(y_ref, wgu_ref, wd_ref, vec_ref, o_ref):
    half = y_ref.shape[0] // 2

    def ln(y, g_row, b_row):
        return _layer_norm(y, vec_ref[g_row:g_row + 1, :], vec_ref[b_row:b_row + 1, :])

    def swiglu(gu):
        gt, up = gu[:, :D_FF], gu[:, D_FF:]
        return ((gt * _sigmoid(gt)) * up).astype(BF16)

    h0 = ln(y_ref[0:half, :], V_LN1G, V_LN1B)
    gu0 = _dot(h0.astype(BF16), wgu_ref[...])
    h1 = ln(y_ref[half:, :], V_LN1G, V_LN1B)
    gu1 = _dot(h1.astype(BF16), wgu_ref[...])
    y0 = _dot(swiglu(gu0), wd_ref[...]) + ALPHA * h0
    act1 = swiglu(gu1)
    o_ref[0:half, :] = ln(y0, V_LN2G, V_LN2B)
    y1 = _dot(act1, wd_ref[...]) + ALPHA * h1
    o_ref[half:, :] = ln(y1, V_LN2G, V_LN2B)


def _ffn_call(h2d, wgu_b, wd_b, vecs, layer, tm):
    rows = h2d.shape[0]
    const = dict(pipeline_mode=pl.Buffered(1))
    weights = 3 * D_MODEL * D_FF * 2
    tile = tm * D_MODEL * 4
    vmem = weights + 4 * tile + 2 * tm * 2 * D_FF * 4 + 2 * tile
    return pl.pallas_call(
        _ffn_kernel,
        grid=(rows // tm,),
        in_specs=[pl.BlockSpec((tm, D_MODEL), lambda i: (i, 0)),
                  pl.BlockSpec((None, D_MODEL, 2 * D_FF), lambda i: (layer, 0, 0), **const),
                  pl.BlockSpec((None, D_FF, D_MODEL), lambda i: (layer, 0, 0), **const),
                  pl.BlockSpec((None, N_VEC, D_MODEL), lambda i: (layer, 0, 0), **const)],
        out_specs=pl.BlockSpec((tm, D_MODEL), lambda i: (i, 0)),
        out_shape=jax.ShapeDtypeStruct((rows, D_MODEL), F32),
        compiler_params=pltpu.CompilerParams(dimension_semantics=("arbitrary",),
                                             vmem_limit_bytes=min(vmem, V7X_VMEM_REQUEST_CAP)),
        name=f"prompt_ffn_{layer}",
    )(h2d, wgu_b, wd_b, vecs)


def _decode_proj_kernel(x_ref, sc_ref, h0_ref, w_in_ref, wri_ref, vec_ref,
                        aout_ref, q_ref, k_ref, v_ref, xq_ref, gate_ref, sconv_ref, sh_ref):
    x = x_ref[...]
    xb = x.astype(BF16)

    def proj(c0, n):
        return _dot(xb, w_in_ref[:, c0:c0 + n])

    xr = proj(C_XR, D_MODEL)
    xc = vec_ref[V_CONV_B:V_CONV_B + 1, :] + vec_ref[V_CONV_W + 3:V_CONV_W + 4, :] * xr
    for k in range(CONV_W - 1):
        xc = xc + vec_ref[V_CONV_W + k:V_CONV_W + k + 1, :] * sc_ref[:, k * D_MODEL:(k + 1) * D_MODEL]
    sconv_ref[:, 0:(CONV_W - 2) * D_MODEL] = sc_ref[:, D_MODEL:(CONV_W - 1) * D_MODEL]
    sconv_ref[:, (CONV_W - 2) * D_MODEL:] = xr

    gelu_y = _gelu_tanh(proj(C_YR, D_MODEL))
    for j in range(N_LRU_GROUPS):
        sl = slice(j * V7X_MXU_DIM, (j + 1) * V7X_MXU_DIM)
        a, u = _lru_gates(xc, wri_ref, vec_ref, j)
        h = a * h0_ref[:, sl] + u
        sh_ref[:, sl] = h
        aout_ref[:, sl] = h * gelu_y[:, sl]

    q_ref[...] = proj(C_Q, D_MODEL)
    k_ref[...] = proj(C_K, KV_W)
    v_ref[...] = proj(C_V, KV_W)
    xq_ref[...] = proj(C_XQ, D_MODEL)
    for n in range(N_BRANCH):
        sl = slice(n * D_MODEL, (n + 1) * D_MODEL)
        gate_ref[:, sl] = _sigmoid(proj(C_G + n * D_MODEL, D_MODEL) + vec_ref[V_BG + n:V_BG + n + 1, :])


def _decode_proj(xs, sc, h0, w_in_b, wri_b, vecs, layer):
    b = xs.shape[0]
    sds = lambda n: jax.ShapeDtypeStruct((b, n), F32)
    full = lambda n: pl.BlockSpec((b, n), lambda i: (0, 0))
    weights = (D_MODEL * IN_COLS + N_LRU_GROUPS * V7X_MXU_DIM * 2 * V7X_MXU_DIM) * 2
    vmem = weights + 2 * 16 * b * D_MODEL * 4 + 8 * b * D_MODEL * 4
    return pl.pallas_call(
        _decode_proj_kernel,
        grid=(1,),
        in_specs=[full(D_MODEL), full((CONV_W - 1) * D_MODEL), full(D_MODEL),
                  pl.BlockSpec((None, D_MODEL, IN_COLS), lambda i: (layer, 0, 0)),
                  pl.BlockSpec((None, N_LRU_GROUPS, V7X_MXU_DIM, 2 * V7X_MXU_DIM), lambda i: (layer, 0, 0, 0)),
                  pl.BlockSpec((None, N_VEC, D_MODEL), lambda i: (layer, 0, 0))],
        out_specs=[full(D_MODEL), full(D_MODEL), full(KV_W), full(KV_W), full(D_MODEL),
                   full(N_BRANCH * D_MODEL), full((CONV_W - 1) * D_MODEL), full(D_MODEL)],
        out_shape=[sds(D_MODEL), sds(D_MODEL), sds(KV_W), sds(KV_W), sds(D_MODEL),
                   sds(N_BRANCH * D_MODEL), sds((CONV_W - 1) * D_MODEL), sds(D_MODEL)],
        compiler_params=pltpu.CompilerParams(dimension_semantics=("arbitrary",),
                                             vmem_limit_bytes=min(vmem, V7X_VMEM_REQUEST_CAP)),
        name=f"decode_proj_{layer}",
    )(xs, sc, h0, w_in_b, wri_b, vecs)


def _decode_attn_kernel(q16_ref, knew_ref, vnew_ref, wk_ref, wv_ref, xq_ref, mk_ref, mv_ref, sink_ref,
                        owk_ref, owv_ref, so_ref, co_ref, *, bt):
    row = lax.broadcasted_iota(jnp.int32, (WINDOW, KV_W), 0)
    head_row = lax.broadcasted_iota(jnp.int32, (N_HEADS, KV_W), 0)
    head_lane = lax.broadcasted_iota(jnp.int32, (N_HEADS, KV_W), 1)
    own = (head_lane >> (HEAD_DIM.bit_length() - 1)) == (head_row >> (GROUP.bit_length() - 1))
    sink = sink_ref[...]
    for i in range(bt):
        kwin = jnp.where(row == WINDOW - 1, knew_ref[i], pltpu.roll(wk_ref[i], WINDOW - 1, 0))
        vwin = jnp.where(row == WINDOW - 1, vnew_ref[i], pltpu.roll(wv_ref[i], WINDOW - 1, 0))
        owk_ref[i] = kwin
        owv_ref[i] = vwin
        q = jnp.where(own, q16_ref[i] * (HEAD_DIM ** -0.5), 0.0).astype(BF16)
        s = _dot_nt(q, kwin.astype(BF16))
        m = jnp.maximum(jnp.max(s, axis=-1, keepdims=True), sink)
        e = jnp.exp(s - m)
        denom = jnp.sum(e, axis=-1, keepdims=True) + jnp.exp(sink - m)
        o = jnp.where(own, _dot(e.astype(BF16), vwin.astype(BF16)) / denom, 0.0)
        acc = o[0:GROUP, :]
        for kk in range(1, N_KV):
            acc = acc + o[kk * GROUP:(kk + 1) * GROUP, :]
        so_ref[i] = acc

        sc = jnp.sum(mk_ref[i] * (xq_ref[i] * (XHEAD_DIM ** -0.5))[None], axis=-1, keepdims=True)
        ex = jnp.exp(sc - jnp.max(sc, axis=0, keepdims=True))
        pr = ex / jnp.sum(ex, axis=0, keepdims=True)
        co_ref[i] = jnp.sum(pr * mv_ref[i], axis=0)


def _decode_attn(q16, knew, vnew, win_k, win_v, xq, mem_k, mem_v, sink_col, layer, bt):
    b = q16.shape[0]
    cache = pl.BlockSpec((None, bt, WINDOW, KV_W), lambda i: (layer, i, 0, 0))
    memsp = pl.BlockSpec((None, bt, N_MEM, N_XHEADS, XHEAD_DIM), lambda i: (layer, i, 0, 0, 0))
    mem_block = bt * N_MEM * V7X_SUBLANES * XHEAD_DIM * 4
    vmem = 2 * (2 * mem_block + bt * (4 * WINDOW * KV_W + 8 * D_MODEL) * 4) + 2 * mem_block
    return pl.pallas_call(
        functools.partial(_decode_attn_kernel, bt=bt),
        grid=(b // bt,),
        in_specs=[pl.BlockSpec((bt, N_HEADS, KV_W), lambda i: (i, 0, 0)),
                  pl.BlockSpec((bt, 1, KV_W), lambda i: (i, 0, 0)),
                  pl.BlockSpec((bt, 1, KV_W), lambda i: (i, 0, 0)),
                  cache, cache,
                  pl.BlockSpec((bt, N_XHEADS, XHEAD_DIM), lambda i: (i, 0, 0)),
                  memsp, memsp,
                  pl.BlockSpec((None, N_HEADS, 1), lambda i: (layer, 0, 0))],
        out_specs=[pl.BlockSpec((bt, WINDOW, KV_W), lambda i: (i, 0, 0)),
                   pl.BlockSpec((bt, WINDOW, KV_W), lambda i: (i, 0, 0)),
                   pl.BlockSpec((bt, GROUP, KV_W), lambda i: (i, 0, 0)),
                   pl.BlockSpec((bt, N_XHEADS, XHEAD_DIM), lambda i: (i, 0, 0))],
        out_shape=[jax.ShapeDtypeStruct((b, WINDOW, KV_W), F32),
                   jax.ShapeDtypeStruct((b, WINDOW, KV_W), F32),
                   jax.ShapeDtypeStruct((b, GROUP, KV_W), F32),
                   jax.ShapeDtypeStruct((b, N_XHEADS, XHEAD_DIM), F32)],
        compiler_params=pltpu.CompilerParams(dimension_semantics=("arbitrary",),
                                             vmem_limit_bytes=min(vmem, V7X_VMEM_REQUEST_CAP)),
        name=f"decode_attn_{layer}",
    )(q16, knew, vnew, win_k, win_v, xq, mem_k, mem_v, sink_col)


def _decode_tail_kernel(x_ref, aout_ref, so_ref, co_ref, gate_ref, wbr_ref, wout_ref, wgu_ref, wd_ref, vec_ref,
                        o_ref):
    merged = None
    for n, br_ref in enumerate((aout_ref, so_ref, co_ref)):
        part = gate_ref[:, n * D_MODEL:(n + 1) * D_MODEL] * _dot(br_ref[...].astype(BF16), wbr_ref[n])
        merged = part if merged is None else merged + part
    h = _merge_tail(x_ref[...], merged, wout_ref, vec_ref)
    o_ref[...] = _ffn(h, wgu_ref, wd_ref, vec_ref)


def _decode_tail(xs, aout, so, co, gates, wbr_b, wout_b, wgu_b, wd_b, vecs, layer):
    b = xs.shape[0]
    full = lambda n: pl.BlockSpec((b, n), lambda i: (0, 0))
    weights = ((N_BRANCH + 1) * D_MODEL * D_MODEL + 3 * D_MODEL * D_FF) * 2
    vmem = weights + 2 * 8 * b * D_MODEL * 4 + 3 * b * 2 * D_FF * 4 + 8 * b * D_MODEL * 4
    return pl.pallas_call(
        _decode_tail_kernel,
        grid=(1,),
        in_specs=[full(D_MODEL), full(D_MODEL), full(D_MODEL), full(D_MODEL), full(N_BRANCH * D_MODEL),
                  pl.BlockSpec((None, N_BRANCH, D_MODEL, D_MODEL), lambda i: (layer, 0, 0, 0)),
                  pl.BlockSpec((None, D_MODEL, D_MODEL), lambda i: (layer, 0, 0)),
                  pl.BlockSpec((None, D_MODEL, 2 * D_FF), lambda i: (layer, 0, 0)),
                  pl.BlockSpec((None, D_FF, D_MODEL), lambda i: (layer, 0, 0)),
                  pl.BlockSpec((None, N_VEC, D_MODEL), lambda i: (layer, 0, 0))],
        out_specs=full(D_MODEL),
        out_shape=jax.ShapeDtypeStruct((b, D_MODEL), F32),
        compiler_params=pltpu.CompilerParams(dimension_semantics=("arbitrary",),
                                             vmem_limit_bytes=min(vmem, V7X_VMEM_REQUEST_CAP)),
        name=f"decode_tail_{layer}",
    )(xs, aout, so, co, gates, wbr_b, wout_b, wgu_b, wd_b, vecs)


def _head_major_to_member_major(w, axis):
    shape = w.shape
    w = w.reshape(shape[:axis] + (N_KV, GROUP, HEAD_DIM) + shape[axis + 1:])
    w = jnp.swapaxes(w, axis, axis + 1)
    return w.reshape(shape)


def _block_diag_groups(w):
    depth = w.shape[0]
    w = w.reshape(depth, N_LRU_GROUPS, LRU_GROUP, LRU_BLOCK, LRU_BLOCK)
    eye = jnp.eye(LRU_GROUP, dtype=w.dtype)
    bd = w[:, :, :, :, None, :] * eye[None, None, :, None, :, None]
    return bd.reshape(depth, N_LRU_GROUPS, V7X_MXU_DIM, V7X_MXU_DIM)


def kernel(x_prompt, x_sample, state_conv, state_h, cache_win_k, cache_win_v, cache_mem_k, cache_mem_v, mem_prompt, w_mem_kv, w_in, b_gates, conv_w, conv_b, lru_wa, lru_ba, lru_wx, lru_bx, lru_lambda, sinks, w_branch, w_out, ln1_g, ln1_b, w_gate_up, w_down, ln2_g, ln2_b):
    bp, seq, _ = x_prompt.shape
    bs = x_sample.shape[0]

    w_in_b = jnp.concatenate(
        [w_in[..., :C_Q], _head_major_to_member_major(w_in[..., C_Q:C_K], 2), w_in[..., C_K:]], axis=-1).astype(BF16)
    wbr_b = jnp.stack(
        [w_branch[:, 0], _head_major_to_member_major(w_branch[:, 1], 1), w_branch[:, 2]], axis=1).astype(BF16)
    wri_b = jnp.concatenate([_block_diag_groups(lru_wa), _block_diag_groups(lru_wx)], axis=-1).astype(BF16)
    wout_b = w_out.astype(BF16)
    wgu_b = w_gate_up.astype(BF16)
    wd_b = w_down.astype(BF16)
    wmem_b = w_mem_kv.astype(BF16)
    row = lambda p: p[:, None, :]
    vecs = jnp.concatenate(
        [conv_w, row(conv_b), row(lru_ba), row(lru_bx), row(lru_lambda), b_gates, row(ln1_g), row(ln1_b),
         row(ln2_g), row(ln2_b), jnp.zeros((DEPTH, N_VEC - 15, D_MODEL), F32)], axis=1)
    sinks_flat = sinks.reshape(DEPTH * N_HEADS)
    sink_col = sinks[:, :, None]

    mk, mv, mk_b, mv_b = _mem_kv(mem_prompt.reshape(bp * N_MEM, D_MODEL), wmem_b, tm=512)
    mk_b = mk_b.reshape(DEPTH, bp, N_MEM, D_MODEL)
    mv_b = mv_b.reshape(DEPTH, bp, N_MEM, D_MODEL)

    win_k = cache_win_k.reshape(DEPTH, bs, WINDOW, KV_W)
    win_v = cache_win_v.reshape(DEPTH, bs, WINDOW, KV_W)

    xp = x_prompt
    xs = x_sample.reshape(bs, D_MODEL)
    p_conv, p_h, p_wk, p_wv = [], [], [], []
    s_conv, s_h, s_wk, s_wv = [], [], [], []
    for l in range(DEPTH):
        h1, pc8, ph, pwk, pwv = _prompt_layer(xp, mk_b, mv_b, w_in_b, wri_b, wbr_b, wout_b, vecs,
                                              sinks_flat[l * N_HEADS:(l + 1) * N_HEADS], l, tm=256)
        xp = _ffn_call(h1.reshape(bp * seq, D_MODEL), wgu_b, wd_b, vecs, l, tm=512).reshape(bp, seq, D_MODEL)
        p_conv.append(pc8[:, V7X_SUBLANES - (CONV_W - 1):, :])
        p_h.append(ph.reshape(bp, D_MODEL))
        p_wk.append(pwk.reshape(bp, WINDOW, N_KV, HEAD_DIM))
        p_wv.append(pwv.reshape(bp, WINDOW, N_KV, HEAD_DIM))

        aout, q, knew, vnew, xq, gates, sconv, sh = _decode_proj(
            xs, state_conv[l].reshape(bs, (CONV_W - 1) * D_MODEL), state_h[l], w_in_b, wri_b, vecs, l)
        q16 = jnp.tile(q.reshape(bs, GROUP, KV_W), (1, N_KV, 1))
        owk, owv, so, co = _decode_attn(q16, knew.reshape(bs, 1, KV_W), vnew.reshape(bs, 1, KV_W), win_k, win_v,
                                        xq.reshape(bs, N_XHEADS, XHEAD_DIM), cache_mem_k, cache_mem_v, sink_col, l, bt=4)
        xs = _decode_tail(xs, aout, so.reshape(bs, D_MODEL), co.reshape(bs, D_MODEL), gates,
                          wbr_b, wout_b, wgu_b, wd_b, vecs, l)
        s_conv.append(sconv.reshape(bs, CONV_W - 1, D_MODEL))
        s_h.append(sh)
        s_wk.append(owk.reshape(bs, WINDOW, N_KV, HEAD_DIM))
        s_wv.append(owv.reshape(bs, WINDOW, N_KV, HEAD_DIM))

    return (xp, xs.reshape(bs, 1, D_MODEL),
            jnp.stack(p_conv), jnp.stack(p_h), jnp.stack(p_wk), jnp.stack(p_wv),
            mk.reshape(DEPTH, bp, N_MEM, N_XHEADS, XHEAD_DIM), mv.reshape(DEPTH, bp, N_MEM, N_XHEADS, XHEAD_DIM),
            jnp.stack(s_conv), jnp.stack(s_h), jnp.stack(s_wk), jnp.stack(s_wv))
```
